```python
import math
import jax
import jax.numpy as jnp
from jax import lax
import numpy as np

D_MODEL = 1024
BATCH = 32
SEQ = 256
DEPTH = 4
DEC_BATCH = 4
DEC_SEQ = 4096
PAST_LEN = 512

GRID_W = 64
N_HEADS_A = 4
HEAD_DK = 128
HEAD_DV = 128
D_A = N_HEADS_A * HEAD_DV
D_B = 512
D_MIX = D_A + D_B
SHORT_CONV = 5
CF_CONV = 31
FFN_CONV = 3
D_FF = 2816
CHUNK = 64
N_MOD = 6
IN_COLS = 4 * D_A + 4 * N_HEADS_A + 2 * D_B
EPS = 1e-6

kernel_name = "hybrid_deltanet_conformer_dit_step"


def rmsnorm(x, g):
    xf = x.astype(jnp.float32)
    y = xf * lax.rsqrt(jnp.mean(xf * xf, axis=-1, keepdims=True) + EPS)
    return (y * g.astype(jnp.float32)).astype(x.dtype)


def layernorm(x, g, b):
    xf = x.astype(jnp.float32)
    mu = jnp.mean(xf, axis=-1, keepdims=True)
    var = jnp.mean(jnp.square(xf - mu), axis=-1, keepdims=True)
    y = (xf - mu) * lax.rsqrt(var + EPS)
    return (y * g.astype(jnp.float32) + b.astype(jnp.float32)).astype(x.dtype)


def l2norm(x):
    return x * lax.rsqrt(jnp.sum(x * x, axis=-1, keepdims=True) + EPS)


def dwconv1d(x, w):
    c_dim = x.shape[-1]
    return lax.conv_general_dilated(
        x, w[:, None, :].astype(x.dtype), window_strides=(1,), padding="SAME",
        dimension_numbers=("NWC", "WIO", "NWC"), feature_group_count=c_dim)


def dwconv2d_grid(x, w):
    b_dim, t_dim, c_dim = x.shape
    rows = t_dim // GRID_W
    xg = x.reshape(b_dim, rows, GRID_W, c_dim)
    y = lax.conv_general_dilated(
        xg, w[:, :, None, :].astype(x.dtype), window_strides=(1, 1), padding="SAME",
        dimension_numbers=("NHWC", "HWIO", "NHWC"), feature_group_count=c_dim)
    return y.reshape(b_dim, t_dim, c_dim)


def gated_delta_chunked(q, k, v, g, beta, s0):
    f32 = jnp.float32
    b_dim, t_dim, h_dim, dk = q.shape
    dv = v.shape[-1]
    n = t_dim // CHUNK

    def to_chunks(t):
        t = t.astype(f32).reshape((b_dim, n, CHUNK, h_dim) + t.shape[3:])
        return jnp.moveaxis(t, 3, 1)

    qc = to_chunks(l2norm(q.astype(f32)) * (dk ** -0.5))
    kc = to_chunks(l2norm(k.astype(f32)))
    vc = to_chunks(v)
    gc = jnp.cumsum(to_chunks(g), axis=-1)
    bc = to_chunks(beta)

    idx = jnp.arange(CHUNK)
    incl = idx[:, None] >= idx[None, :]
    strict = idx[:, None] > idx[None, :]
    diff = gc[..., :, None] - gc[..., None, :]
    decay = jnp.where(incl, jnp.exp(jnp.where(incl, diff, 0.0)), 0.0)
    kb = kc * bc[..., None]
    lmat = jnp.where(strict, jnp.einsum("bhncd,bhnsd->bhncs", kb, kc) * decay, 0.0)
    amat = lmat + jnp.eye(CHUNK, dtype=f32)
    rhs = jnp.concatenate([vc * bc[..., None], kb * jnp.exp(gc)[..., None]], axis=-1)
    sol = lax.linalg.triangular_solve(amat, rhs, left_side=True, lower=True,
                                      unit_diagonal=True)
    uc, wc = sol[..., :dv], sol[..., dv:]
    qk = jnp.einsum("bhncd,bhnsd->bhncs", qc, kc) * decay

    def step(s, inp):
        q_i, k_i, u_i, w_i, g_i, qk_i = inp
        v_new = u_i - jnp.einsum("bhck,bhkv->bhcv", w_i, s)
        o_i = (jnp.einsum("bhck,bhkv->bhcv", q_i * jnp.exp(g_i)[..., None], s)
               + jnp.einsum("bhcs,bhsv->bhcv", qk_i, v_new))
        g_last = g_i[..., -1:]
        s = (s * jnp.exp(g_last)[..., None]
             + jnp.einsum("bhck,bhcv->bhkv", k_i * jnp.exp(g_last - g_i)[..., None], v_new))
        return s, o_i

    xs = tuple(jnp.moveaxis(t, 2, 0) for t in (qc, kc, uc, wc, gc, qk))
    s_fin, oc = lax.scan(step, s0.astype(f32), xs)
    o = jnp.transpose(oc, (1, 0, 3, 2, 4)).reshape(b_dim, t_dim, h_dim, dv)
    return o, s_fin


def trunk_layer(x, mod, s_init, grid, w_in, conv_qkv, a_log, dt_bias, g_onorm,
                conv_cf, b_conv_cf, ln_cf_g, ln_cf_b, w_out, w_up, conv_ffn, b_conv_ffn,
                w_down, g_pre_mix, g_post_mix, g_pre_ffn, g_post_ffn):
    b_dim, t_dim, _ = x.shape
    f32 = jnp.float32
    shift_m, scale_m, gate_m, shift_f, scale_f, gate_f = jnp.split(mod[:, None, :], N_MOD, axis=-1)

    h = rmsnorm(x, g_pre_mix) * (1.0 + scale_m) + shift_m
    proj = h @ w_in
    qkv, z, ab, glu = jnp.split(proj, [3 * D_A, 4 * D_A, 4 * D_A + 4 * N_HEADS_A], axis=-1)

    qkv = jax.nn.silu(dwconv1d(qkv, conv_qkv))
    q, k, v = jnp.split(qkv, 3, axis=-1)
    q = q.reshape(b_dim, t_dim, N_HEADS_A, HEAD_DK)
    k = k.reshape(b_dim, t_dim, N_HEADS_A, HEAD_DK)
    v = v.reshape(b_dim, t_dim, N_HEADS_A, HEAD_DV)
    a_raw = ab[..., :2 * N_HEADS_A].reshape(b_dim, t_dim, 2, N_HEADS_A).astype(f32)
    b_raw = ab[..., 2 * N_HEADS_A:].reshape(b_dim, t_dim, 2, N_HEADS_A).astype(f32)
    g = -jnp.exp(a_log.astype(f32)) * jax.nn.softplus(a_raw + dt_bias.astype(f32))
    beta = jax.nn.sigmoid(b_raw)
    o_fwd, s_fwd = gated_delta_chunked(q, k, v, g[:, :, 0], beta[:, :, 0], s_init[:, 0])
    o_bwd, s_bwd = gated_delta_chunked(q[:, ::-1], k[:, ::-1], v[:, ::-1],
                                       g[:, ::-1, 1], beta[:, ::-1, 1], s_init[:, 1])
    o = o_fwd + o_bwd[:, ::-1]
    o = o * lax.rsqrt(jnp.mean(o * o, axis=-1, keepdims=True) + EPS) * g_onorm.astype(f32)
    o = o * jax.nn.silu(z.reshape(b_dim, t_dim, N_HEADS_A, HEAD_DV).astype(f32))
    o = o.reshape(b_dim, t_dim, D_A).astype(x.dtype)

    glu_a, glu_b = jnp.split(glu, 2, axis=-1)
    u = glu_a * jax.nn.sigmoid(glu_b)
    u = dwconv1d(u, conv_cf) + b_conv_cf
    u = jax.nn.silu(layernorm(u, ln_cf_g, ln_cf_b))

    mix = jnp.concatenate([o, u], axis=-1) @ w_out
    x = x + gate_m * rmsnorm(mix, g_post_mix)

    h = rmsnorm(x, g_pre_ffn) * (1.0 + scale_f) + shift_f
    gt, up = jnp.split(h @ w_up, 2, axis=-1)
    gt = (dwconv2d_grid(gt, conv_ffn) if grid else dwconv1d(gt, conv_ffn[1])) + b_conv_ffn
    y = (jax.nn.silu(gt) * up) @ w_down
    x = x + gate_f * rmsnorm(y, g_post_ffn)
    return x, jnp.stack([s_fwd, s_bwd], axis=1)


def setup_inputs(seed: int = 0) -> dict:
    key = jax.random.key(seed)
    ks = jax.random.split(key, 32)
    f32 = jnp.float32
    nrm = lambda k_, shape, s: jax.random.normal(k_, shape, f32) * s
    gain = lambda k_, shape: 1.0 + nrm(k_, shape, 0.02)
    return {
        "x_prompt": nrm(ks[0], (BATCH, SEQ, D_MODEL), 1.0),
        "x_sample": nrm(ks[1], (DEC_BATCH, DEC_SEQ, D_MODEL), 1.0),
        "state_delta": nrm(ks[2], (DEC_BATCH, DEPTH, 2, N_HEADS_A, HEAD_DK, HEAD_DV), 0.5),
        "c": nrm(ks[3], (DEC_BATCH, D_MODEL), 1.0),
        "c_ctx": nrm(ks[4], (D_MODEL,), 1.0),
        "w_mod": nrm(ks[5], (DEPTH, D_MODEL, N_MOD * D_MODEL), 0.5 * D_MODEL ** -0.5),
        "b_mod": nrm(ks[6], (DEPTH, N_MOD * D_MODEL), 0.02),
        "g_pre_mix": gain(ks[7], (DEPTH, D_MODEL)),
        "g_post_mix": gain(ks[8], (DEPTH, D_MODEL)),
        "g_pre_ffn": gain(ks[9], (DEPTH, D_MODEL)),
        "g_post_ffn": gain(ks[10], (DEPTH, D_MODEL)),
        "w_in": nrm(ks[11], (DEPTH, D_MODEL, IN_COLS), D_MODEL ** -0.5),
        "conv_qkv": nrm(ks[12], (DEPTH, SHORT_CONV, 3 * D_A), SHORT_CONV ** -0.5),
        "a_log": jnp.log(jax.random.uniform(ks[13], (DEPTH, 2, N_HEADS_A), f32, 1.0, 16.0)),
        "dt_bias": -3.0 + nrm(ks[14], (DEPTH, 2, N_HEADS_A), 0.5),
        "g_onorm": gain(ks[15], (DEPTH, HEAD_DV)),
        "conv_cf": nrm(ks[16], (DEPTH, CF_CONV, D_B), CF_CONV ** -0.5),
        "b_conv_cf": nrm(ks[17], (DEPTH, D_B), 0.02),
        "ln_cf_g": gain(ks[18], (DEPTH, D_B)),
        "ln_cf_b": nrm(ks[19], (DEPTH, D_B), 0.02),
        "w_out": nrm(ks[20], (DEPTH, D_MIX, D_MODEL), D_MIX ** -0.5),
        "w_up": nrm(ks[21], (DEPTH, D_MODEL, 2 * D_FF), D_MODEL ** -0.5),
        "conv_ffn": nrm(ks[22], (DEPTH, FFN_CONV, FFN_CONV, D_FF), 1.0 / FFN_CONV),
        "b_conv_ffn": nrm(ks[23], (DEPTH, D_FF), 0.02),
        "w_down": nrm(ks[24], (DEPTH, D_FF, D_MODEL), D_FF ** -0.5),
    }


def reference(x_prompt, x_sample, state_delta, c, c_ctx, w_mod, b_mod, g_pre_mix, g_post_mix,
              g_pre_ffn, g_post_ffn, w_in, conv_qkv, a_log, dt_bias, g_onorm, conv_cf,
              b_conv_cf, ln_cf_g, ln_cf_b, w_out, w_up, conv_ffn, b_conv_ffn, w_down):
    xp = x_prompt
    xs = x_sample
    s_zero = jnp.zeros((xp.shape[0], 2, N_HEADS_A, HEAD_DK, HEAD_DV), jnp.float32)
    ctx_states = []
    for l in range(DEPTH):
        mod_ctx = (jax.nn.silu(c_ctx) @ w_mod[l] + b_mod[l])[None, :]
        mod_lat = jax.nn.silu(c) @ w_mod[l] + b_mod[l]
        weights = (w_in[l], conv_qkv[l], a_log[l], dt_bias[l], g_onorm[l], conv_cf[l],
                   b_conv_cf[l], ln_cf_g[l], ln_cf_b[l], w_out[l], w_up[l], conv_ffn[l],
                   b_conv_ffn[l], w_down[l], g_pre_mix[l], g_post_mix[l], g_pre_ffn[l],
                   g_post_ffn[l])
        xp, s_ctx = trunk_layer(xp, mod_ctx, s_zero, False, *weights)
        ctx_states.append(s_ctx)
        xs, _ = trunk_layer(xs, mod_lat, state_delta[:, l], True, *weights)
    new_state_delta = jnp.stack(ctx_states, axis=1)
    return (xp, xs, new_state_delta)
```

```python
import functools

import jax
import jax.numpy as jnp
from jax import lax
from jax.experimental import pallas as pl
from jax.experimental.pallas import tpu as pltpu

F32 = jnp.float32
MM = jnp.bfloat16
EPS = 1e-6
GRID_W = 64
N_HEADS = 4
HEAD_D = 128
D_A = N_HEADS * HEAD_D
D_B = 512
SHORT_CONV = 5
CF_CONV = 31
CHUNK = 64
N_MOD = 6
HALO = 16
FF_CHUNK = 256
LANES = 128
ROWS = 64
VMEM_LIMIT = 56 * 1024 * 1024


def _sigmoid(x):
    return 1.0 / (1.0 + jnp.exp(-x))


def _silu(x):
    return x * _sigmoid(x)


def _softplus(x):
    return jnp.maximum(x, 0.0) + jnp.log(1.0 + jnp.exp(-jnp.abs(x)))


def _dot(a, b):
    return jnp.dot(a.astype(MM), b.astype(MM), preferred_element_type=F32)


def _dot_nt(a, b):
    return lax.dot_general(a.astype(MM), b.astype(MM), (((1,), (1,)), ((), ())),
                           preferred_element_type=F32)


def _dot_tn(a, b):
    return lax.dot_general(a.astype(MM), b.astype(MM), (((0,), (0,)), ((), ())),
                           preferred_element_type=F32)


def _dot_exact(a, b):
    return jnp.dot(a, b, preferred_element_type=F32, precision=lax.Precision.HIGHEST)


def _rms_scale(x):
    return lax.rsqrt(jnp.mean(x * x, axis=-1, keepdims=True) + EPS)


def _mod_kernel(c_ref, w_ref, b_ref, o_ref):
    o_ref[0] = _dot_exact(_silu(c_ref[...]), w_ref[0]) + b_ref[0]


def _modulation(cc, w_mod, b_mod):
    depth, d, n = w_mod.shape
    tn = 1536
    return pl.pallas_call(
        _mod_kernel,
        grid=(depth, n // tn),
        in_specs=[
            pl.BlockSpec((8, d), lambda l, j: (0, 0)),
            pl.BlockSpec((1, d, tn), lambda l, j: (l, 0, j)),
            pl.BlockSpec((1, 1, tn), lambda l, j: (l, 0, j)),
        ],
        out_specs=pl.BlockSpec((1, 8, tn), lambda l, j: (l, 0, j)),
        out_shape=jax.ShapeDtypeStruct((depth, 8, n), F32),
        compiler_params=pltpu.CompilerParams(vmem_limit_bytes=VMEM_LIMIT),
        name="modulation",
    )(cc, w_mod, b_mod.reshape(depth, 1, n))


def _layer_spec(a, layer):
    return pl.BlockSpec((1,) + a.shape[1:], lambda b, i: (layer,) + (0,) * (a.ndim - 1))


def _mixer_in_kernel(xp_ref, xc_ref, xn_ref, mod_ref, gpre_ref, wc_ref, wr_ref, cq_ref,
                     ccf_ref, bcf_ref, lng_ref, lnb_ref, alog_ref, dt_ref,
                     q_ref, k_ref, v_ref, z_ref, gb_ref, u_ref,
                     hext_ref, pext_ref, *, tm, nt):
    i = pl.program_id(1)
    shift = mod_ref[0, 0:1, :]
    scale1 = 1.0 + mod_ref[0, 1:2, :]
    gpre = gpre_ref[0]
    rb = ROWS

    def norm(x):
        return ((x * _rms_scale(x)) * gpre) * scale1 + shift

    hext_ref[0:HALO, :] = norm(xp_ref[0]).astype(MM)
    hext_ref[HALO + tm:HALO + tm + HALO, :] = norm(xn_ref[0]).astype(MM)
    for r in range(0, tm, rb):
        hext_ref[HALO + r:HALO + r + rb, :] = norm(xc_ref[0, r:r + rb, :]).astype(MM)

    for r in range(0, tm, rb):
        rest = jnp.dot(hext_ref[HALO + r:HALO + r + rb, :], wr_ref[0],
                       preferred_element_type=F32)
        z_ref[0, r:r + rb, :] = rest[:, :D_A]
        ab = rest[:, D_A:]
        g = -jnp.exp(alog_ref[0]) * _softplus(ab + dt_ref[0])
        col = lax.broadcasted_iota(jnp.int32, ab.shape, 1)
        gb_ref[0, r:r + rb, :] = jnp.where(col < 2 * N_HEADS, g, _sigmoid(ab))

    n_ext = tm + 2 * HALO
    for r in range(0, n_ext, rb):
        rows = min(rb, n_ext - r)
        proj = jnp.dot(hext_ref[r:r + rows, :], wc_ref[0], preferred_element_type=F32)
        row = r + lax.broadcasted_iota(jnp.int32, (rows, 1), 0)
        valid = jnp.logical_and(jnp.logical_or(row >= HALO, i > 0),
                                jnp.logical_or(row < HALO + tm, i < nt - 1))
        proj = jnp.where(valid, proj, 0.0)
        pext_ref[r:r + rows, 0:3 * D_A] = proj[:, 0:3 * D_A]
        glu_a = proj[:, 3 * D_A:3 * D_A + D_B]
        glu_b = proj[:, 3 * D_A + D_B:]
        pext_ref[r:r + rows, 3 * D_A:] = glu_a * _sigmoid(glu_b)

    half = SHORT_CONV // 2
    for r in range(0, tm, rb):
        for part, out_ref in enumerate((q_ref, k_ref, v_ref)):
            c0 = part * D_A
            acc = None
            for t in range(SHORT_CONV):
                s = HALO + r + t - half
                term = pext_ref[s:s + rb, c0:c0 + D_A] * cq_ref[0, t:t + 1, c0:c0 + D_A]
                acc = term if acc is None else acc + term
            acc = _silu(acc)
            if part < 2:
                heads = []
                for h in range(N_HEADS):
                    a = acc[:, h * HEAD_D:(h + 1) * HEAD_D]
                    a = a * lax.rsqrt(jnp.sum(a * a, axis=-1, keepdims=True) + EPS)
                    heads.append(a * (HEAD_D ** -0.5) if part == 0 else a)
                acc = jnp.concatenate(heads, axis=-1)
            out_ref[0, r:r + rb, :] = acc

    half = CF_CONV // 2
    for r in range(0, tm, rb):
        acc = None
        for t in range(CF_CONV):
            s = HALO + r + t - half
            term = pext_ref[s:s + rb, 3 * D_A:] * ccf_ref[0, t:t + 1, :]
            acc = term if acc is None else acc + term
        acc = acc + bcf_ref[0]
        mu = jnp.mean(acc, axis=-1, keepdims=True)
        cen = acc - mu
        var = jnp.mean(cen * cen, axis=-1, keepdims=True)
        y = cen * lax.rsqrt(var + EPS) * lng_ref[0] + lnb_ref[0]
        u_ref[0, r:r + rb, :] = _silu(y).astype(u_ref.dtype)


def _mixer_in(x, mod, mrow, layer, wts, *, tm):
    bsz, t, d = x.shape
    nt = t // tm
    hb = tm // HALO
    nhb = t // HALO
    names = ("g_pre_mix", "w_conv", "w_rest", "conv_qkv", "conv_cf", "b_conv_cf", "ln_cf_g",
             "ln_cf_b", "a_log", "dt_bias")
    seq = lambda c: pl.BlockSpec((1, tm, c), lambda b, i: (b, i, 0))
    return pl.pallas_call(
        functools.partial(_mixer_in_kernel, tm=tm, nt=nt),
        grid=(bsz, nt),
        in_specs=[
            pl.BlockSpec((1, HALO, d), lambda b, i: (b, jnp.maximum(i * hb - 1, 0), 0)),
            pl.BlockSpec((1, tm, d), lambda b, i: (b, i, 0)),
            pl.BlockSpec((1, HALO, d), lambda b, i: (b, jnp.minimum((i + 1) * hb, nhb - 1), 0)),
            pl.BlockSpec((1, N_MOD, d), lambda b, i: (mrow(b), 0, 0)),
        ] + [_layer_spec(wts[n], layer) for n in names],
        out_specs=[seq(D_A), seq(D_A), seq(D_A), seq(D_A), seq(LANES), seq(D_B)],
        out_shape=[
            jax.ShapeDtypeStruct((bsz, t, D_A), F32),
            jax.ShapeDtypeStruct((bsz, t, D_A), F32),
            jax.ShapeDtypeStruct((bsz, t, D_A), F32),
            jax.ShapeDtypeStruct((bsz, t, D_A), F32),
            jax.ShapeDtypeStruct((bsz, t, LANES), F32),
            jax.ShapeDtypeStruct((bsz, t, D_B), MM),
        ],
        scratch_shapes=[
            pltpu.VMEM((tm + 2 * HALO, d), MM),
            pltpu.VMEM((tm + 2 * HALO, 3 * D_A + D_B), F32),
        ],
        compiler_params=pltpu.CompilerParams(vmem_limit_bytes=VMEM_LIMIT),
        name="mixer_in",
    )(x, x, x, mod, *[wts[n] for n in names])


def _dn_chunk(q_ref, k_ref, v_ref, gb_ref, o_ref, s_ref, r0, d):
    n = N_HEADS * CHUNK
    rows = slice(r0, r0 + CHUNK)
    gbc = gb_ref[0, rows, :]
    ci = lax.broadcasted_iota(jnp.int32, (CHUNK, CHUNK), 0)
    cj = lax.broadcasted_iota(jnp.int32, (CHUNK, CHUNK), 1)
    tri = (ci >= cj) if d == 0 else (ci <= cj)
    gc = _dot_exact(jnp.where(tri, 1.0, 0.0), gbc)
    last = CHUNK - 1 if d == 0 else 0
    gtot = gc[last:last + 1, :]
    egc = jnp.exp(gc)
    edk = jnp.exp(gtot - gc)
    egt = jnp.exp(gtot)
    gct = jnp.concatenate([gc, jnp.zeros_like(gc)], axis=0).T

    def stack_cols(a, c0):
        return jnp.concatenate([a[:, c0 + h:c0 + h + 1] for h in range(N_HEADS)], axis=0)

    c0 = d * N_HEADS
    gc_col = stack_cols(gc, c0)
    be_col = stack_cols(gbc, 2 * N_HEADS + c0)
    egc_col = stack_cols(egc, c0)
    edk_col = stack_cols(edk, c0)
    gc_row = jnp.concatenate([gct[c0 + h:c0 + h + 1, 0:CHUNK] for h in range(N_HEADS)], axis=1)

    def stack_heads(ref):
        return jnp.concatenate(
            [ref[0, rows, h * HEAD_D:(h + 1) * HEAD_D] for h in range(N_HEADS)], axis=0)

    qs, ks, vs = stack_heads(q_ref), stack_heads(k_ref), stack_heads(v_ref)

    ii = lax.broadcasted_iota(jnp.int32, (n, n), 0)
    jj = lax.broadcasted_iota(jnp.int32, (n, n), 1)
    lo, hi = (jj, ii) if d == 0 else (ii, jj)
    log2_chunk = CHUNK.bit_length() - 1
    same = (ii >> log2_chunk) == (jj >> log2_chunk)
    incl = jnp.logical_and(same, lo <= hi)
    decay = jnp.where(incl, jnp.exp(jnp.where(incl, gc_col - gc_row, 0.0)), 0.0)

    kb = ks * be_col
    amat = _dot_nt(kb, ks) * decay
    pair = jnp.logical_and((lo >> 1) == (hi >> 1), lo < hi)
    tinv = jnp.where(ii == jj, 1.0, 0.0) - jnp.where(pair, amat, 0.0)
    for lb in range(1, log2_chunk):
        e_mask = jnp.logical_and((lo >> (lb + 1)) == (hi >> (lb + 1)), (lo >> lb) < (hi >> lb))
        te = _dot(tinv, jnp.where(e_mask, amat, 0.0))
        tinv = tinv - _dot(te, tinv)

    uw = _dot(tinv, jnp.concatenate([vs * be_col, kb * egc_col], axis=1))
    u, w = uw[:, :HEAD_D], uw[:, HEAD_D:]
    qk = _dot_nt(qs, ks) * decay
    qg = qs * egc_col
    kd = ks * edk_col

    vnew, o1 = [], []
    for h in range(N_HEADS):
        hs = slice(h * CHUNK, (h + 1) * CHUNK)
        r = _dot(jnp.concatenate([w[hs], qg[hs]], axis=0), s_ref[d, h])
        vnew.append(u[hs] - r[:CHUNK])
        o1.append(r[CHUNK:])
    vnew = jnp.concatenate(vnew, axis=0)
    o = jnp.concatenate(o1, axis=0) + _dot(qk, vnew)
    for h in range(N_HEADS):
        hs = slice(h * CHUNK, (h + 1) * CHUNK)
        s_ref[d, h] = s_ref[d, h] * egt[:, c0 + h:c0 + h + 1] + _dot_tn(kd[hs], vnew[hs])
        o_ref[0, rows, h * HEAD_D:(h + 1) * HEAD_D] = o[hs]


def _deltanet_kernel(*refs, nc, zero_init):
    if zero_init:
        qf, kf, vf, gf, qb, kb, vb, gb, of_ref, ob_ref, sfin_ref, s_ref = refs
    else:
        qf, kf, vf, gf, qb, kb, vb, gb, s0_ref, of_ref, ob_ref, sfin_ref, s_ref = refs
    j = pl.program_id(1)

    @pl.when(j == 0)
    def _():
        if zero_init:
            s_ref[...] = jnp.zeros_like(s_ref)
        else:
            s_ref[...] = s0_ref[0]

    for c in range(nc):
        _dn_chunk(qf, kf, vf, gf, of_ref, s_ref, c * CHUNK, 0)
        _dn_chunk(qb, kb, vb, gb, ob_ref, s_ref, (nc - 1 - c) * CHUNK, 1)

    @pl.when(j == pl.num_programs(1) - 1)
    def _():
        sfin_ref[0] = s_ref[...]


def _deltanet(q, k, v, gb, s0, *, tt):
    bsz, t, _ = q.shape
    nt = t // tt
    zero_init = s0 is None
    fwd = lambda c: pl.BlockSpec((1, tt, c), lambda b, j: (b, j, 0))
    bwd = lambda c: pl.BlockSpec((1, tt, c), lambda b, j: (b, nt - 1 - j, 0))
    st_spec = pl.BlockSpec((1, 2, N_HEADS, HEAD_D, HEAD_D), lambda b, j: (b, 0, 0, 0, 0))
    in_specs = [fwd(D_A), fwd(D_A), fwd(D_A), fwd(LANES), bwd(D_A), bwd(D_A), bwd(D_A), bwd(LANES)]
    args = [q, k, v, gb, q, k, v, gb]
    if not zero_init:
        in_specs.append(st_spec)
        args.append(s0)
    return pl.pallas_call(
        functools.partial(_deltanet_kernel, nc=tt // CHUNK, zero_init=zero_init),
        grid=(bsz, nt),
        in_specs=in_specs,
        out_specs=[fwd(D_A), bwd(D_A), st_spec],
        out_shape=[
            jax.ShapeDtypeStruct((bsz, t, D_A), F32),
            jax.ShapeDtypeStruct((bsz, t, D_A), F32),
            jax.ShapeDtypeStruct((bsz, 2, N_HEADS, HEAD_D, HEAD_D), F32),
        ],
        scratch_shapes=[pltpu.VMEM((2, N_HEADS, HEAD_D, HEAD_D), F32)],
        compiler_params=pltpu.CompilerParams(
            dimension_semantics=("parallel", "arbitrary"), vmem_limit_bytes=VMEM_LIMIT),
        name="deltanet",
    )(*args)


def _mixer_out_kernel(of_ref, ob_ref, z_ref, u_ref, x_ref, mod_ref, gon_ref, wo_ref, gpost_ref,
                      out_ref, cat_ref, *, tm):
    rb = ROWS
    gate = mod_ref[0, 2:3, :]
    for r in range(0, tm, rb):
        rows = slice(r, r + rb)
        o = of_ref[0, rows, :] + ob_ref[0, rows, :]
        z = z_ref[0, rows, :]
        for h in range(N_HEADS):
            hs = slice(h * HEAD_D, (h + 1) * HEAD_D)
            oh = o[:, hs]
            oh = oh * _rms_scale(oh) * gon_ref[0]
            cat_ref[rows, hs] = (oh * _silu(z[:, hs])).astype(MM)
        cat_ref[rows, D_A:] = u_ref[0, rows, :]
    for r in range(0, tm, rb):
        rows = slice(r, r + rb)
        mix = jnp.dot(cat_ref[rows, :], wo_ref[0], preferred_element_type=F32)
        mix = mix * _rms_scale(mix) * gpost_ref[0]
        out_ref[0, rows, :] = x_ref[0, rows, :] + gate * mix


def _mixer_out(o_f, o_b, z, u, x, mod, mrow, layer, wts, *, tm):
    bsz, t, d = x.shape
    names = ("g_onorm", "w_out", "g_post_mix")
    seq = lambda c: pl.BlockSpec((1, tm, c), lambda b, i: (b, i, 0))
    return pl.pallas_call(
        functools.partial(_mixer_out_kernel, tm=tm),
        grid=(bsz, t // tm),
        in_specs=[
            seq(D_A), seq(D_A), seq(D_A), seq(D_B), seq(d),
            pl.BlockSpec((1, N_MOD, d), lambda b, i: (mrow(b), 0, 0)),
        ] + [_layer_spec(wts[n], layer) for n in names],
        out_specs=seq(d),
        out_shape=jax.ShapeDtypeStruct((bsz, t, d), F32),
        scratch_shapes=[pltpu.VMEM((tm, D_A + D_B), MM)],
        compiler_params=pltpu.CompilerParams(vmem_limit_bytes=VMEM_LIMIT),
        name="mixer_out",
    )(o_f, o_b, z, u, x, mod, *[wts[n] for n in names])


def _ffn_kernel(xp_ref, xc_ref, xn_ref, mod_ref, gpre_ref, wg_ref, wu_ref, wd_ref, cw_ref,
                cb_ref, gpost_ref, out_ref, hext_ref, gts_ref, y_ref,
                *, tm, nt, halo, width, taps, n_chunks):
    i = pl.program_id(1)
    pad = 8
    shift = mod_ref[0, 3:4, :]
    scale1 = 1.0 + mod_ref[0, 4:5, :]
    gate = mod_ref[0, 5:6, :]
    gpre = gpre_ref[0]
    rb = ROWS

    def norm(x):
        return ((x * _rms_scale(x)) * gpre) * scale1 + shift

    if halo:
        hext_ref[0:halo, :] = norm(xp_ref[0]).astype(MM)
        hext_ref[halo + tm:halo + tm + halo, :] = norm(xn_ref[0]).astype(MM)
    for r in range(0, tm, rb):
        hext_ref[halo + r:halo + r + rb, :] = norm(xc_ref[0, r:r + rb, :]).astype(MM)

    n_ext = tm + 2 * halo
    gts_ref[0:pad, :] = jnp.zeros((pad, FF_CHUNK), F32)
    gts_ref[pad + n_ext:pad + n_ext + pad, :] = jnp.zeros((pad, FF_CHUNK), F32)
    y_ref[...] = jnp.zeros_like(y_ref)

    def chunk_body(c, carry):
        wg = wg_ref[0, c]
        for r in range(0, n_ext, rb):
            gt = jnp.dot(hext_ref[r:r + rb, :], wg, preferred_element_type=F32)
            if halo and r < halo:
                gt = jnp.where(i > 0, gt, 0.0)
            if halo and r >= halo + tm:
                gt = jnp.where(i < nt - 1, gt, 0.0)
            gts_ref[pad + r:pad + r + rb, :] = gt
        wu = wu_ref[0, c]
        wd = wd_ref[0, c]
        for r in range(0, tm, rb):
            pos = (r + lax.broadcasted_iota(jnp.int32, (rb, 1), 0)) % width
            acc = None
            for dr, dc, widx in taps:
                s = pad + halo + r + dr * GRID_W + dc
                val = gts_ref[s:s + rb, :]
                if dc != 0:
                    ok = (pos >= 1) if dc < 0 else (pos <= width - 2)
                    val = jnp.where(ok, val, 0.0)
                term = val * cw_ref[0, c, widx:widx + 1, :]
                acc = term if acc is None else acc + term
            gt = acc + cb_ref[0, c]
            up = jnp.dot(hext_ref[halo + r:halo + r + rb, :], wu, preferred_element_type=F32)
            act = (_silu(gt) * up).astype(MM)
            y_ref[r:r + rb, :] += jnp.dot(act, wd, preferred_element_type=F32)
        return carry

    lax.fori_loop(0, n_chunks, chunk_body, 0)

    for r in range(0, tm, rb):
        y = y_ref[r:r + rb, :]
        y = y * _rms_scale(y) * gpost_ref[0]
        out_ref[0, r:r + rb, :] = xc_ref[0, r:r + rb, :] + gate * y


def _ffn(x, mod, mrow, layer, wts, *, tm, grid_conv):
    bsz, t, d = x.shape
    nt = t // tm
    n_chunks = wts["w_gate"].shape[1]
    if grid_conv:
        halo = GRID_W
        width = GRID_W
        taps = tuple((dr, dc, (dr + 1) * 3 + (dc + 1)) for dr in (-1, 0, 1) for dc in (-1, 0, 1))
        hb, nhb = tm // halo, t // halo
        xp_spec = pl.BlockSpec((1, halo, d), lambda b, i: (b, jnp.maximum(i * hb - 1, 0), 0))
        xn_spec = pl.BlockSpec((1, halo, d), lambda b, i: (b, jnp.minimum((i + 1) * hb, nhb - 1), 0))
    else:
        assert tm == t
        halo = 0
        width = t
        taps = tuple((0, dc, 3 + (dc + 1)) for dc in (-1, 0, 1))
        xp_spec = pl.BlockSpec((1, 8, d), lambda b, i: (b, 0, 0))
        xn_spec = pl.BlockSpec((1, 8, d), lambda b, i: (b, 0, 0))
    names = ("g_pre_ffn", "w_gate", "w_upp", "w_down", "conv_ffn", "b_conv_ffn", "g_post_ffn")
    return pl.pallas_call(
        functools.partial(_ffn_kernel, tm=tm, nt=nt, halo=halo, width=width, taps=taps,
                          n_chunks=n_chunks),
        grid=(bsz, nt),
        in_specs=[
            xp_spec,
            pl.BlockSpec((1, tm, d), lambda b, i: (b, i, 0)),
            xn_spec,
            pl.BlockSpec((1, N_MOD, d), lambda b, i: (mrow(b), 0, 0)),
        ] + [_layer_spec(wts[n], layer) for n in names],
        out_specs=pl.BlockSpec((1, tm, d), lambda b, i: (b, i, 0)),
        out_shape=jax.ShapeDtypeStruct((bsz, t, d), F32),
        scratch_shapes=[
            pltpu.VMEM((tm + 2 * halo, d), MM),
            pltpu.VMEM((tm + 2 * halo + 16, FF_CHUNK), F32),
            pltpu.VMEM((tm, d), F32),
        ],
        compiler_params=pltpu.CompilerParams(vmem_limit_bytes=VMEM_LIMIT),
        name="ffn",
    )(x, x, x, mod, *[wts[n] for n in names])


def _prepare_weights(w_in, conv_qkv, a_log, dt_bias, g_onorm, conv_cf, b_conv_cf, ln_cf_g,
                     ln_cf_b, w_out, w_up, conv_ffn, b_conv_ffn, w_down, g_pre_mix, g_post_mix,
                     g_pre_ffn, g_post_ffn):
    depth, d, _ = w_in.shape
    d_ff = w_down.shape[1]
    n_chunks = d_ff // FF_CHUNK
    n_ab = 4 * N_HEADS
    qkv = w_in[:, :, :3 * D_A]
    zc = w_in[:, :, 3 * D_A:4 * D_A]
    ab = w_in[:, :, 4 * D_A:4 * D_A + n_ab]
    glu = w_in[:, :, 4 * D_A + n_ab:]
    vec = lambda a: a.reshape(depth, 1, -1)
    pad_lane = lambda a: jnp.pad(a.reshape(depth, 1, -1),
                                 ((0, 0), (0, 0), (0, LANES - 2 * N_HEADS)))
    chunked = lambda w: jnp.transpose(w.reshape(depth, d, n_chunks, FF_CHUNK), (0, 2, 1, 3))
    return {
        "w_conv": jnp.concatenate([qkv, glu], axis=-1).astype(MM),
        "w_rest": jnp.concatenate(
            [zc, ab, jnp.zeros((depth, d, LANES - n_ab), w_in.dtype)], axis=-1).astype(MM),
        "conv_qkv": conv_qkv, "conv_cf": conv_cf, "b_conv_cf": vec(b_conv_cf),
        "ln_cf_g": vec(ln_cf_g), "ln_cf_b": vec(ln_cf_b),
        "a_log": pad_lane(a_log), "dt_bias": pad_lane(dt_bias),
        "g_onorm": vec(g_onorm), "w_out": w_out.astype(MM),
        "w_gate": chunked(w_up[:, :, :d_ff]).astype(MM),
        "w_upp": chunked(w_up[:, :, d_ff:]).astype(MM),
        "w_down": w_down.reshape(depth, n_chunks, FF_CHUNK, d).astype(MM),
        "conv_ffn": jnp.transpose(
            conv_ffn.reshape(depth, 9, n_chunks, FF_CHUNK), (0, 2, 1, 3)),
        "b_conv_ffn": b_conv_ffn.reshape(depth, n_chunks, 1, FF_CHUNK),
        "g_pre_mix": vec(g_pre_mix), "g_post_mix": vec(g_post_mix),
        "g_pre_ffn": vec(g_pre_ffn), "g_post_ffn": vec(g_post_ffn),
    }


def _trunk_layer(x, mod, mrow, s0, layer, wts, *, tm_in, tt, tm_out, tm_ffn, grid_conv):
    q, k, v, z, gb, u = _mixer_in(x, mod, mrow, layer, wts, tm=tm_in)
    o_f, o_b, s_fin = _deltanet(q, k, v, gb, s0, tt=tt)
    x = _mixer_out(o_f, o_b, z, u, x, mod, mrow, layer, wts, tm=tm_out)
    x = _ffn(x, mod, mrow, layer, wts, tm=tm_ffn, grid_conv=grid_conv)
    return x, s_fin


def kernel(x_prompt, x_sample, state_delta, c, c_ctx, w_mod, b_mod, g_pre_mix, g_post_mix,
           g_pre_ffn, g_post_ffn, w_in, conv_qkv, a_log, dt_bias, g_onorm, conv_cf, b_conv_cf,
           ln_cf_g, ln_cf_b, w_out, w_up, conv_ffn, b_conv_ffn, w_down):
    depth, d = g_pre_mix.shape
    dec_b = c.shape[0]
    assert dec_b < 8
    cc = jnp.concatenate([c, c_ctx[None, :], jnp.zeros((8 - dec_b - 1, d), c.dtype)], axis=0)
    mod = _modulation(cc, w_mod, b_mod).reshape(depth, 8, N_MOD, d)
    wts = _prepare_weights(w_in, conv_qkv, a_log, dt_bias, g_onorm, conv_cf, b_conv_cf, ln_cf_g,
                           ln_cf_b, w_out, w_up, conv_ffn, b_conv_ffn, w_down, g_pre_mix,
                           g_post_mix, g_pre_ffn, g_post_ffn)
    xp, xs = x_prompt, x_sample
    seq = xp.shape[1]
    lat_tile = min(512, xs.shape[1])
    ctx_states = []
    for layer in range(depth):
        xp, s_ctx = _trunk_layer(xp, mod[layer], lambda b: dec_b, None, layer, wts,
                                 tm_in=seq, tt=seq, tm_out=seq, tm_ffn=seq, grid_conv=False)
        ctx_states.append(s_ctx)
        xs, _ = _trunk_layer(xs, mod[layer], lambda b: b, state_delta[:, layer], layer, wts,
                             tm_in=lat_tile, tt=256, tm_out=lat_tile, tm_ffn=lat_tile,
                             grid_conv=True)
    return xp, xs, jnp.stack(ctx_states, axis=1)
```

```python
import functools

import jax
import jax.numpy as jnp
from jax import lax
from jax.experimental import pallas as pl
from jax.experimental.pallas import tpu as pltpu

F32 = jnp.float32
MM = jnp.bfloat16
EPS = 1e-6
GRID_W = 64
N_HEADS = 4
HEAD_D = 128
D_A = N_HEADS * HEAD_D
D_B = 512
SHORT_CONV = 5
CF_CONV = 31
CHUNK = 64
N_MOD = 6
HALO = 16
FF_CHUNK = 256
LANES = 128
ROWS = 64
VMEM_LIMIT = 56 * 1024 * 1024


def _sigmoid(x):
    return 1.0 / (1.0 + jnp.exp(-x))


def _silu(x):
    return x * _sigmoid(x)


def _softplus(x):
    return jnp.maximum(x, 0.0) + jnp.log(1.0 + jnp.exp(-jnp.abs(x)))


def _dot(a, b):
    return jnp.dot(a.astype(MM), b.astype(MM), preferred_element_type=F32)


def _dot_nt(a, b):
    return lax.dot_general(a.astype(MM), b.astype(MM), (((1,), (1,)), ((), ())),
                           preferred_element_type=F32)


def _dot_tn(a, b):
    return lax.dot_general(a.astype(MM), b.astype(MM), (((0,), (0,)), ((), ())),
                           preferred_element_type=F32)


def _dot_exact(a, b):
    return jnp.dot(a, b, preferred_element_type=F32, precision=lax.Precision.HIGHEST)


def _rms_scale(x):
    return lax.rsqrt(jnp.mean(x * x, axis=-1, keepdims=True) + EPS)


def _mod_kernel(c_ref, w_ref, b_ref, o_ref):
    o_ref[0] = _dot_exact(_silu(c_ref[...]), w_ref[0]) + b_ref[0]


def _modulation(cc, w_mod, b_mod):
    depth, d, n = w_mod.shape
    tn = 1536
    return pl.pallas_call(
        _mod_kernel,
        grid=(depth, n // tn),
        in_specs=[
            pl.BlockSpec((8, d), lambda l, j: (0, 0)),
            pl.BlockSpec((1, d, tn), lambda l, j: (l, 0, j)),
            pl.BlockSpec((1, 1, tn), lambda l, j: (l, 0, j)),
        ],
        out_specs=pl.BlockSpec((1, 8, tn), lambda l, j: (l, 0, j)),
        out_shape=jax.ShapeDtypeStruct((depth, 8, n), F32),
        compiler_params=pltpu.CompilerParams(vmem_limit_bytes=VMEM_LIMIT),
        name="modulation",
    )(cc, w_mod, b_mod.reshape(depth, 1, n))


def _layer_spec(a, layer):
    return pl.BlockSpec((1,) + a.shape[1:], lambda b, i: (layer,) + (0,) * (a.ndim - 1))


def _mixer_in_kernel(xp_ref, xc_ref, xn_ref, mod_ref, gpre_ref, wqkv_ref, wglu_ref, wz_ref,
                     wab_ref, cq_ref, ccf_ref, bcf_ref, lng_ref, lnb_ref, alog_ref, dt_ref,
                     q_ref, k_ref, v_ref, z_ref, gb_ref, u_ref,
                     hext_ref, pext_ref, glu_ref, *, tm, nt):
    i = pl.program_id(1)
    shift = mod_ref[0, 0:1, :]
    scale1 = 1.0 + mod_ref[0, 1:2, :]
    gpre = gpre_ref[0]
    rb = ROWS
    n_ext = tm + 2 * HALO

    def norm(x):
        return ((x * _rms_scale(x)) * gpre) * scale1 + shift

    hext_ref[0:HALO, :] = jnp.where(i > 0, norm(xp_ref[0]), 0.0).astype(MM)
    hext_ref[HALO + tm:n_ext, :] = jnp.where(i < nt - 1, norm(xn_ref[0]), 0.0).astype(MM)
    for r in range(0, tm, rb):
        hext_ref[HALO + r:HALO + r + rb, :] = norm(xc_ref[0, r:r + rb, :]).astype(MM)

    hcur = hext_ref[HALO:HALO + tm, :]
    z_ref[0] = jnp.dot(hcur, wz_ref[0], preferred_element_type=F32)
    ab = jnp.dot(hcur, wab_ref[0], preferred_element_type=F32)
    g = -jnp.exp(alog_ref[0]) * _softplus(ab + dt_ref[0])
    col = lax.broadcasted_iota(jnp.int32, ab.shape, 1)
    gb_ref[0] = jnp.where(col < 2 * N_HEADS, g, _sigmoid(ab))

    pext_ref[:, 0:3 * D_A] = jnp.dot(hext_ref[...], wqkv_ref[0], preferred_element_type=F32)
    glu_ref[...] = jnp.dot(hext_ref[...], wglu_ref[0], preferred_element_type=F32)
    for r in range(0, n_ext, rb):
        rows = min(rb, n_ext - r)
        pext_ref[r:r + rows, 3 * D_A:] = (glu_ref[r:r + rows, 0:D_B]
                                          * _sigmoid(glu_ref[r:r + rows, D_B:]))

    half = SHORT_CONV // 2
    for r in range(0, tm, rb):
        for part, out_ref in enumerate((q_ref, k_ref, v_ref)):
            c0 = part * D_A
            acc = None
            for t in range(SHORT_CONV):
                s = HALO + r + t - half
                term = pext_ref[s:s + rb, c0:c0 + D_A] * cq_ref[0, t:t + 1, c0:c0 + D_A]
                acc = term if acc is None else acc + term
            acc = _silu(acc)
            if part < 2:
                heads = []
                for h in range(N_HEADS):
                    a = acc[:, h * HEAD_D:(h + 1) * HEAD_D]
                    a = a * lax.rsqrt(jnp.sum(a * a, axis=-1, keepdims=True) + EPS)
                    heads.append(a * (HEAD_D ** -0.5) if part == 0 else a)
                acc = jnp.concatenate(heads, axis=-1)
            out_ref[0, r:r + rb, :] = acc

    half = CF_CONV // 2
    for r in range(0, tm, rb):
        acc = None
        for t in range(CF_CONV):
            s = HALO + r + t - half
            term = pext_ref[s:s + rb, 3 * D_A:] * ccf_ref[0, t:t + 1, :]
            acc = term if acc is None else acc + term
        acc = acc + bcf_ref[0]
        mu = jnp.mean(acc, axis=-1, keepdims=True)
        cen = acc - mu
        var = jnp.mean(cen * cen, axis=-1, keepdims=True)
        y = cen * lax.rsqrt(var + EPS) * lng_ref[0] + lnb_ref[0]
        u_ref[0, r:r + rb, :] = _silu(y).astype(u_ref.dtype)


def _mixer_in(x, mod, mrow, layer, wts, *, tm):
    bsz, t, d = x.shape
    nt = t // tm
    hb = tm // HALO
    nhb = t // HALO
    names = ("g_pre_mix", "w_qkv", "w_glu", "w_z", "w_ab", "conv_qkv", "conv_cf", "b_conv_cf",
             "ln_cf_g", "ln_cf_b", "a_log", "dt_bias")
    seq = lambda c: pl.BlockSpec((1, tm, c), lambda b, i: (b, i, 0))
    return pl.pallas_call(
        functools.partial(_mixer_in_kernel, tm=tm, nt=nt),
        grid=(bsz, nt),
        in_specs=[
            pl.BlockSpec((1, HALO, d), lambda b, i: (b, jnp.maximum(i * hb - 1, 0), 0)),
            pl.BlockSpec((1, tm, d), lambda b, i: (b, i, 0)),
            pl.BlockSpec((1, HALO, d), lambda b, i: (b, jnp.minimum((i + 1) * hb, nhb - 1), 0)),
            pl.BlockSpec((1, N_MOD, d), lambda b, i: (mrow(b), 0, 0)),
        ] + [_layer_spec(wts[n], layer) for n in names],
        out_specs=[seq(D_A), seq(D_A), seq(D_A), seq(D_A), seq(LANES), seq(D_B)],
        out_shape=[
            jax.ShapeDtypeStruct((bsz, t, D_A), F32),
            jax.ShapeDtypeStruct((bsz, t, D_A), F32),
            jax.ShapeDtypeStruct((bsz, t, D_A), F32),
            jax.ShapeDtypeStruct((bsz, t, D_A), F32),
            jax.ShapeDtypeStruct((bsz, t, LANES), F32),
            jax.ShapeDtypeStruct((bsz, t, D_B), MM),
        ],
        scratch_shapes=[
            pltpu.VMEM((tm + 2 * HALO, d), MM),
            pltpu.VMEM((tm + 2 * HALO, 3 * D_A + D_B), F32),
            pltpu.VMEM((tm + 2 * HALO, 2 * D_B), F32),
        ],
        compiler_params=pltpu.CompilerParams(vmem_limit_bytes=VMEM_LIMIT),
        name="mixer_in",
    )(x, x, x, mod, *[wts[n] for n in names])


def _dn_masks(d):
    n = N_HEADS * CHUNK
    ii = lax.broadcasted_iota(jnp.int32, (n, n), 0)
    jj = lax.broadcasted_iota(jnp.int32, (n, n), 1)
    lo, hi = (jj, ii) if d == 0 else (ii, jj)
    return ii, jj, lo, hi


def _dn_prep(q_ref, k_ref, v_ref, gb_ref, r0, d, p, amat_ref, tinv_ref, qk_ref, rhs_ref,
             qg_ref, kd_ref, egt_ref):
    rows = slice(r0, r0 + CHUNK)
    log2_chunk = CHUNK.bit_length() - 1
    gbc = gb_ref[0, rows, :]
    ci = lax.broadcasted_iota(jnp.int32, (CHUNK, CHUNK), 0)
    cj = lax.broadcasted_iota(jnp.int32, (CHUNK, CHUNK), 1)
    tri = (ci >= cj) if d == 0 else (ci <= cj)
    gc = _dot_exact(jnp.where(tri, 1.0, 0.0), gbc)
    last = CHUNK - 1 if d == 0 else 0
    gtot = gc[last:last + 1, :]
    egc = jnp.exp(gc)
    edk = jnp.exp(gtot - gc)
    egt_ref[p] = jnp.exp(gtot)
    gct = jnp.concatenate([gc, jnp.zeros_like(gc)], axis=0).T

    def stack_cols(a, c0):
        return jnp.concatenate([a[:, c0 + h:c0 + h + 1] for h in range(N_HEADS)], axis=0)

    c0 = d * N_HEADS
    gc_col = stack_cols(gc, c0)
    be_col = stack_cols(gbc, 2 * N_HEADS + c0)
    egc_col = stack_cols(egc, c0)
    edk_col = stack_cols(edk, c0)
    gc_row = jnp.concatenate([gct[c0 + h:c0 + h + 1, 0:CHUNK] for h in range(N_HEADS)], axis=1)

    def stack_heads(ref):
        return jnp.concatenate(
            [ref[0, rows, h * HEAD_D:(h + 1) * HEAD_D] for h in range(N_HEADS)], axis=0)

    qs, ks, vs = stack_heads(q_ref), stack_heads(k_ref), stack_heads(v_ref)
    ii, jj, lo, hi = _dn_masks(d)
    same = (ii >> log2_chunk) == (jj >> log2_chunk)
    incl = jnp.logical_and(same, lo <= hi)
    decay = jnp.where(incl, jnp.exp(jnp.where(incl, gc_col - gc_row, 0.0)), 0.0)
    kb = ks * be_col
    amat = _dot_nt(kb, ks) * decay
    amat_ref[p] = amat
    pair = jnp.logical_and((lo >> 1) == (hi >> 1), lo < hi)
    tinv_ref[p] = jnp.where(ii == jj, 1.0, 0.0) - jnp.where(pair, amat, 0.0)
    qk_ref[p] = (_dot_nt(qs, ks) * decay).astype(MM)
    rhs_ref[p] = jnp.concatenate([vs * be_col, kb * egc_col], axis=1).astype(MM)
    qg_ref[p] = (qs * egc_col).astype(MM)
    kd_ref[p] = (ks * edk_col).astype(MM)


def _deltanet_kernel(*refs, nc, zero_init):
    if zero_init:
        qf, kf, vf, gf, qb, kb, vb, gb, of_ref, ob_ref, sfin_ref = refs[:11]
    else:
        qf, kf, vf, gf, qb, kb, vb, gb, s0_ref, of_ref, ob_ref, sfin_ref = refs[:12]
    amat_ref, tinv_ref, qk_ref, rhs_ref, qg_ref, kd_ref, egt_ref, uw_ref, s_ref = refs[-9:]
    j = pl.program_id(1)
    log2_chunk = CHUNK.bit_length() - 1

    @pl.when(j == 0)
    def _():
        if zero_init:
            s_ref[...] = jnp.zeros_like(s_ref)
        else:
            s_ref[...] = s0_ref[0]

    probs = []
    for c in range(nc):
        probs.append((0, c * CHUNK, (qf, kf, vf, gf), of_ref))
        probs.append((1, (nc - 1 - c) * CHUNK, (qb, kb, vb, gb), ob_ref))

    for p, (d, r0, (q_ref, k_ref, v_ref, g_ref), _) in enumerate(probs):
        _dn_prep(q_ref, k_ref, v_ref, g_ref, r0, d, p, amat_ref, tinv_ref, qk_ref, rhs_ref,
                 qg_ref, kd_ref, egt_ref)

    for lb in range(1, log2_chunk):
        e_masks = []
        for d in (0, 1):
            _, _, lo, hi = _dn_masks(d)
            e_masks.append(jnp.logical_and((lo >> (lb + 1)) == (hi >> (lb + 1)),
                                           (lo >> lb) < (hi >> lb)))
        for p, (d, _, _, _) in enumerate(probs):
            tb = tinv_ref[p].astype(MM)
            te = _dot(tb, jnp.where(e_masks[d], amat_ref[p], 0.0))
            tinv_ref[p] = tinv_ref[p] - _dot(te, tb)

    for p in range(len(probs)):
        uw_ref[p] = _dot(tinv_ref[p], rhs_ref[p])

    for p, (d, r0, _, o_ref) in enumerate(probs):
        c0 = d * N_HEADS
        vnew, o1 = [], []
        for h in range(N_HEADS):
            hs = slice(h * CHUNK, (h + 1) * CHUNK)
            wq = jnp.concatenate([uw_ref[p, hs, HEAD_D:].astype(MM), qg_ref[p, hs, :]], axis=0)
            r = _dot(wq, s_ref[d, h])
            vnew.append(uw_ref[p, hs, 0:HEAD_D] - r[:CHUNK])
            o1.append(r[CHUNK:])
        vnew = jnp.concatenate(vnew, axis=0).astype(MM)
        o = jnp.concatenate(o1, axis=0) + _dot(qk_ref[p], vnew)
        for h in range(N_HEADS):
            hs = slice(h * CHUNK, (h + 1) * CHUNK)
            s_ref[d, h] = (s_ref[d, h] * egt_ref[p, :, c0 + h:c0 + h + 1]
                           + _dot_tn(kd_ref[p, hs, :], vnew[hs]))
            o_ref[0, r0:r0 + CHUNK, h * HEAD_D:(h + 1) * HEAD_D] = o[hs]

    @pl.when(j == pl.num_programs(1) - 1)
    def _():
        sfin_ref[0] = s_ref[...]


def _deltanet(q, k, v, gb, s0, *, tt):
    bsz, t, _ = q.shape
    nt = t // tt
    nc = tt // CHUNK
    n = N_HEADS * CHUNK
    zero_init = s0 is None
    fwd = lambda c: pl.BlockSpec((1, tt, c), lambda b, j: (b, j, 0))
    bwd = lambda c: pl.BlockSpec((1, tt, c), lambda b, j: (b, nt - 1 - j, 0))
    st_spec = pl.BlockSpec((1, 2, N_HEADS, HEAD_D, HEAD_D), lambda b, j: (b, 0, 0, 0, 0))
    in_specs = [fwd(D_A), fwd(D_A), fwd(D_A), fwd(LANES), bwd(D_A), bwd(D_A), bwd(D_A), bwd(LANES)]
    args = [q, k, v, gb, q, k, v, gb]
    if not zero_init:
        in_specs.append(st_spec)
        args.append(s0)
    n_prob = 2 * nc
    return pl.pallas_call(
        functools.partial(_deltanet_kernel, nc=nc, zero_init=zero_init),
        grid=(bsz, nt),
        in_specs=in_specs,
        out_specs=[fwd(D_A), bwd(D_A), st_spec],
        out_shape=[
            jax.ShapeDtypeStruct((bsz, t, D_A), F32),
            jax.ShapeDtypeStruct((bsz, t, D_A), F32),
            jax.ShapeDtypeStruct((bsz, 2, N_HEADS, HEAD_D, HEAD_D), F32),
        ],
        scratch_shapes=[
            pltpu.VMEM((n_prob, n, n), F32),
            pltpu.VMEM((n_prob, n, n), F32),
            pltpu.VMEM((n_prob, n, n), MM),
            pltpu.VMEM((n_prob, n, 2 * HEAD_D), MM),
            pltpu.VMEM((n_prob, n, HEAD_D), MM),
            pltpu.VMEM((n_prob, n, HEAD_D), MM),
            pltpu.VMEM((n_prob, 1, LANES), F32),
            pltpu.VMEM((n_prob, n, 2 * HEAD_D), F32),
            pltpu.VMEM((2, N_HEADS, HEAD_D, HEAD_D), F32),
        ],
        compiler_params=pltpu.CompilerParams(
            dimension_semantics=("parallel", "arbitrary"), vmem_limit_bytes=VMEM_LIMIT),
        name="deltanet",
    )(*args)


def _mixer_out_kernel(of_ref, ob_ref, z_ref, u_ref, x_ref, mod_ref, gon_ref, wo_ref, gpost_ref,
                      out_ref, cat_ref, mix_ref, *, tm):
    rb = ROWS
    gate = mod_ref[0, 2:3, :]
    for r in range(0, tm, rb):
        rows = slice(r, r + rb)
        o = of_ref[0, rows, :] + ob_ref[0, rows, :]
        z = z_ref[0, rows, :]
        for h in range(N_HEADS):
            hs = slice(h * HEAD_D, (h + 1) * HEAD_D)
            oh = o[:, hs]
            oh = oh * _rms_scale(oh) * gon_ref[0]
            cat_ref[rows, hs] = (oh * _silu(z[:, hs])).astype(MM)
        cat_ref[rows, D_A:] = u_ref[0, rows, :]
    mix_ref[...] = jnp.dot(cat_ref[...], wo_ref[0], preferred_element_type=F32)
    for r in range(0, tm, rb):
        rows = slice(r, r + rb)
        mix = mix_ref[rows, :]
        mix = mix * _rms_scale(mix) * gpost_ref[0]
        out_ref[0, rows, :] = x_ref[0, rows, :] + gate * mix


def _mixer_out(o_f, o_b, z, u, x, mod, mrow, layer, wts, *, tm):
    bsz, t, d = x.shape
    names = ("g_onorm", "w_out", "g_post_mix")
    seq = lambda c: pl.BlockSpec((1, tm, c), lambda b, i: (b, i, 0))
    return pl.pallas_call(
        functools.partial(_mixer_out_kernel, tm=tm),
        grid=(bsz, t // tm),
        in_specs=[
            seq(D_A), seq(D_A), seq(D_A), seq(D_B), seq(d),
            pl.BlockSpec((1, N_MOD, d), lambda b, i: (mrow(b), 0, 0)),
        ] + [_layer_spec(wts[n], layer) for n in names],
        out_specs=seq(d),
        out_shape=jax.ShapeDtypeStruct((bsz, t, d), F32),
        scratch_shapes=[pltpu.VMEM((tm, D_A + D_B), MM), pltpu.VMEM((tm, d), F32)],
        compiler_params=pltpu.CompilerParams(vmem_limit_bytes=VMEM_LIMIT),
        name="mixer_out",
    )(o_f, o_b, z, u, x, mod, *[wts[n] for n in names])


def _ffn_kernel(xp_ref, xc_ref, xn_ref, mod_ref, gpre_ref, wg_ref, wu_ref, wd_ref, cw_ref,
                cb_ref, gpost_ref, out_ref, hext_ref, gts_ref, up_ref, act_ref, y_ref,
                *, tm, nt, halo, width, taps, n_chunks):
    i = pl.program_id(1)
    pad = 8
    shift = mod_ref[0, 3:4, :]
    scale1 = 1.0 + mod_ref[0, 4:5, :]
    gate = mod_ref[0, 5:6, :]
    gpre = gpre_ref[0]
    rb = ROWS
    n_ext = tm + 2 * halo

    def norm(x):
        return ((x * _rms_scale(x)) * gpre) * scale1 + shift

    if halo:
        hext_ref[0:halo, :] = jnp.where(i > 0, norm(xp_ref[0]), 0.0).astype(MM)
        hext_ref[halo + tm:n_ext, :] = jnp.where(i < nt - 1, norm(xn_ref[0]), 0.0).astype(MM)
    for r in range(0, tm, rb):
        hext_ref[halo + r:halo + r + rb, :] = norm(xc_ref[0, r:r + rb, :]).astype(MM)

    gts_ref[0:pad, :] = jnp.zeros((pad, FF_CHUNK), F32)
    gts_ref[pad + n_ext:pad + n_ext + pad, :] = jnp.zeros((pad, FF_CHUNK), F32)
    y_ref[...] = jnp.zeros_like(y_ref)

    def chunk_body(c, carry):
        gts_ref[pad:pad + n_ext, :] = jnp.dot(hext_ref[...], wg_ref[0, c],
                                              preferred_element_type=F32)
        up_ref[...] = jnp.dot(hext_ref[halo:halo + tm, :], wu_ref[0, c],
                              preferred_element_type=F32)
        for r in range(0, tm, rb):
            pos = (r + lax.broadcasted_iota(jnp.int32, (rb, 1), 0)) % width
            acc = None
            for dr, dc, widx in taps:
                s = pad + halo + r + dr * GRID_W + dc
                val = gts_ref[s:s + rb, :]
                if dc != 0:
                    ok = (pos >= 1) if dc < 0 else (pos <= width - 2)
                    val = jnp.where(ok, val, 0.0)
                term = val * cw_ref[0, c, widx:widx + 1, :]
                acc = term if acc is None else acc + term
            gt = acc + cb_ref[0, c]
            act_ref[r:r + rb, :] = (_silu(gt) * up_ref[r:r + rb, :]).astype(MM)
        y_ref[...] += jnp.dot(act_ref[...], wd_ref[0, c], preferred_element_type=F32)
        return carry

    lax.fori_loop(0, n_chunks, chunk_body, 0)

    for r in range(0, tm, rb):
        y = y_ref[r:r + rb, :]
        y = y * _rms_scale(y) * gpost_ref[0]
        out_ref[0, r:r + rb, :] = xc_ref[0, r:r + rb, :] + gate * y


def _ffn(x, mod, mrow, layer, wts, *, tm, grid_conv):
    bsz, t, d = x.shape
    nt = t // tm
    n_chunks = wts["w_gate"].shape[1]
    if grid_conv:
        halo = GRID_W
        width = GRID_W
        taps = tuple((dr, dc, (dr + 1) * 3 + (dc + 1)) for dr in (-1, 0, 1) for dc in (-1, 0, 1))
        hb, nhb = tm // halo, t // halo
        xp_spec = pl.BlockSpec((1, halo, d), lambda b, i: (b, jnp.maximum(i * hb - 1, 0), 0))
        xn_spec = pl.BlockSpec((1, halo, d), lambda b, i: (b, jnp.minimum((i + 1) * hb, nhb - 1), 0))
    else:
        assert tm == t
        halo = 0
        width = t
        taps = tuple((0, dc, 3 + (dc + 1)) for dc in (-1, 0, 1))
        xp_spec = pl.BlockSpec((1, 8, d), lambda b, i: (b, 0, 0))
        xn_spec = pl.BlockSpec((1, 8, d), lambda b, i: (b, 0, 0))
    names = ("g_pre_ffn", "w_gate", "w_upp", "w_down", "conv_ffn", "b_conv_ffn", "g_post_ffn")
    return pl.pallas_call(
        functools.partial(_ffn_kernel, tm=tm, nt=nt, halo=halo, width=width, taps=taps,
                          n_chunks=n_chunks),
        grid=(bsz, nt),
        in_specs=[
            xp_spec,
            pl.BlockSpec((1, tm, d), lambda b, i: (b, i, 0)),
            xn_spec,
            pl.BlockSpec((1, N_MOD, d), lambda b, i: (mrow(b), 0, 0)),
        ] + [_layer_spec(wts[n], layer) for n in names],
        out_specs=pl.BlockSpec((1, tm, d), lambda b, i: (b, i, 0)),
        out_shape=jax.ShapeDtypeStruct((bsz, t, d), F32),
        scratch_shapes=[
            pltpu.VMEM((tm + 2 * halo, d), MM),
            pltpu.VMEM((tm + 2 * halo + 16, FF_CHUNK), F32),
            pltpu.VMEM((tm, FF_CHUNK), F32),
            pltpu.VMEM((tm, FF_CHUNK), MM),
            pltpu.VMEM((tm, d), F32),
        ],
        compiler_params=pltpu.CompilerParams(vmem_limit_bytes=VMEM_LIMIT),
        name="ffn",
    )(x, x, x, mod, *[wts[n] for n in names])


def _prepare_weights(w_in, conv_qkv, a_log, dt_bias, g_onorm, conv_cf, b_conv_cf, ln_cf_g,
                     ln_cf_b, w_out, w_up, conv_ffn, b_conv_ffn, w_down, g_pre_mix, g_post_mix,
                     g_pre_ffn, g_post_ffn):
    depth, d, _ = w_in.shape
    d_ff = w_down.shape[1]
    n_chunks = d_ff // FF_CHUNK
    n_ab = 4 * N_HEADS
    ab = w_in[:, :, 4 * D_A:4 * D_A + n_ab]
    vec = lambda a: a.reshape(depth, 1, -1)
    pad_lane = lambda a: jnp.pad(a.reshape(depth, 1, -1),
                                 ((0, 0), (0, 0), (0, LANES - 2 * N_HEADS)))
    chunked = lambda w: jnp.transpose(w.reshape(depth, d, n_chunks, FF_CHUNK), (0, 2, 1, 3))
    return {
        "w_qkv": w_in[:, :, :3 * D_A].astype(MM),
        "w_z": w_in[:, :, 3 * D_A:4 * D_A].astype(MM),
        "w_ab": jnp.pad(ab, ((0, 0), (0, 0), (0, LANES - n_ab))).astype(MM),
        "w_glu": w_in[:, :, 4 * D_A + n_ab:].astype(MM),
        "conv_qkv": conv_qkv, "conv_cf": conv_cf, "b_conv_cf": vec(b_conv_cf),
        "ln_cf_g": vec(ln_cf_g), "ln_cf_b": vec(ln_cf_b),
        "a_log": pad_lane(a_log), "dt_bias": pad_lane(dt_bias),
        "g_onorm": vec(g_onorm), "w_out": w_out.astype(MM),
        "w_gate": chunked(w_up[:, :, :d_ff]).astype(MM),
        "w_upp": chunked(w_up[:, :, d_ff:]).astype(MM),
        "w_down": w_down.reshape(depth, n_chunks, FF_CHUNK, d).astype(MM),
        "conv_ffn": jnp.transpose(
            conv_ffn.reshape(depth, 9, n_chunks, FF_CHUNK), (0, 2, 1, 3)),
        "b_conv_ffn": b_conv_ffn.reshape(depth, n_chunks, 1, FF_CHUNK),
        "g_pre_mix": vec(g_pre_mix), "g_post_mix": vec(g_post_mix),
        "g_pre_ffn": vec(g_pre_ffn), "g_post_ffn": vec(g_post_ffn),
    }


def _trunk_layer(x, mod, mrow, s0, layer, wts, *, tm_in, tt, tm_out, tm_ffn, grid_conv):
    q, k, v, z, gb, u = _mixer_in(x, mod, mrow, layer, wts, tm=tm_in)
    o_f, o_b, s_fin = _deltanet(q, k, v, gb, s0, tt=tt)
    x = _mixer_out(o_f, o_b, z, u, x, mod, mrow, layer, wts, tm=tm_out)
    x = _ffn(x, mod, mrow, layer, wts, tm=tm_ffn, grid_conv=grid_conv)
    return x, s_fin


def kernel(x_prompt, x_sample, state_delta, c, c_ctx, w_mod, b_mod, g_pre_mix, g_post_mix,
           g_pre_ffn, g_post_ffn, w_in, conv_qkv, a_log, dt_bias, g_onorm, conv_cf, b_conv_cf,
           ln_cf_g, ln_cf_b, w_out, w_up, conv_ffn, b_conv_ffn, w_down):
    depth, d = g_pre_mix.shape
    dec_b = c.shape[0]
    assert dec_b < 8
    cc = jnp.concatenate([c, c_ctx[None, :], jnp.zeros((8 - dec_b - 1, d), c.dtype)], axis=0)
    mod = _modulation(cc, w_mod, b_mod).reshape(depth, 8, N_MOD, d)
    wts = _prepare_weights(w_in, conv_qkv, a_log, dt_bias, g_onorm, conv_cf, b_conv_cf, ln_cf_g,
                           ln_cf_b, w_out, w_up, conv_ffn, b_conv_ffn, w_down, g_pre_mix,
                           g_post_mix, g_pre_ffn, g_post_ffn)
    xp, xs = x_prompt, x_sample
    seq = xp.shape[1]
    lat_tile = min(512, xs.shape[1])
    ctx_states = []
    for layer in range(depth):
        xp, s_ctx = _trunk_layer(xp, mod[layer], lambda b: dec_b, None, layer, wts,
                                 tm_in=seq, tt=seq, tm_out=seq, tm_ffn=seq, grid_conv=False)
        ctx_states.append(s_ctx)
        xs, _ = _trunk_layer(xs, mod[layer], lambda b: b, state_delta[:, layer], layer, wts,
                             tm_in=lat_tile, tt=256, tm_out=lat_tile, tm_ffn=lat_tile,
                             grid_conv=True)
    return xp, xs, jnp.stack(ctx_states, axis=1)
```

```python
import functools

import jax
import jax.numpy as jnp
from jax import lax
from jax.experimental import pallas as pl
from jax.experimental.pallas import tpu as pltpu

F32 = jnp.float32
MM = jnp.bfloat16
EPS = 1e-6
GRID_W = 64
N_HEADS = 4
HEAD_D = 128
D_A = N_HEADS * HEAD_D
D_B = 512
SHORT_CONV = 5
CF_CONV = 31
CHUNK = 64
N_MOD = 6
HALO = 16
FF_CHUNK = 256
LANES = 128
SUBLANES = 8
SH_COLS = 512
ROWS = 64
ACT_ROWS = 64
VMEM_LIMIT = 56 * 1024 * 1024


def _sigmoid(x):
    return 1.0 / (1.0 + jnp.exp(-x))


def _silu(x):
    return x * _sigmoid(x)


def _softplus(x):
    return jnp.maximum(x, 0.0) + jnp.log(1.0 + jnp.exp(-jnp.abs(x)))


def _dot(a, b):
    return jnp.dot(a.astype(MM), b.astype(MM), preferred_element_type=F32)


def _dot_nt(a, b):
    return lax.dot_general(a.astype(MM), b.astype(MM), (((1,), (1,)), ((), ())),
                           preferred_element_type=F32)


def _dot_tn(a, b):
    return lax.dot_general(a.astype(MM), b.astype(MM), (((0,), (0,)), ((), ())),
                           preferred_element_type=F32)


def _dot_exact(a, b):
    return jnp.dot(a, b, preferred_element_type=F32, precision=lax.Precision.HIGHEST)


def _rms_scale(x):
    return lax.rsqrt(jnp.mean(x * x, axis=-1, keepdims=True) + EPS)


def _mod_kernel(c_ref, w_ref, b_ref, o_ref):
    o_ref[0] = _dot_exact(_silu(c_ref[...]), w_ref[0]) + b_ref[0]


def _modulation(cc, w_mod, b_mod):
    depth, d, n = w_mod.shape
    tn = 1536
    return pl.pallas_call(
        _mod_kernel,
        grid=(depth, n // tn),
        in_specs=[
            pl.BlockSpec((8, d), lambda l, j: (0, 0)),
            pl.BlockSpec((1, d, tn), lambda l, j: (l, 0, j)),
            pl.BlockSpec((1, 1, tn), lambda l, j: (l, 0, j)),
        ],
        out_specs=pl.BlockSpec((1, 8, tn), lambda l, j: (l, 0, j)),
        out_shape=jax.ShapeDtypeStruct((depth, 8, n), F32),
        compiler_params=pltpu.CompilerParams(vmem_limit_bytes=VMEM_LIMIT),
        name="modulation",
    )(cc, w_mod, b_mod.reshape(depth, 1, n))


def _layer_spec(a, layer):
    return pl.BlockSpec((1,) + a.shape[1:], lambda b, i: (layer,) + (0,) * (a.ndim - 1))


def _mixer_in_kernel(xp_ref, xc_ref, xn_ref, mod_ref, gpre_ref, wqkv_ref, wglu_ref, wz_ref,
                     wab_ref, cq_ref, ccf_ref, bcf_ref, lng_ref, lnb_ref, alog_ref, dt_ref,
                     q_ref, k_ref, v_ref, z_ref, gb_ref, u_ref,
                     hext_ref, pext_ref, glu_ref, sh_ref, *, tm, nt):
    i = pl.program_id(1)
    shift = mod_ref[0, 0:1, :]
    scale1 = 1.0 + mod_ref[0, 1:2, :]
    gpre = gpre_ref[0]
    rb = ROWS
    n_ext = tm + 2 * HALO

    def norm(x):
        return ((x * _rms_scale(x)) * gpre) * scale1 + shift

    hext_ref[0:HALO, :] = jnp.where(i > 0, norm(xp_ref[0]), 0.0).astype(MM)
    hext_ref[HALO + tm:n_ext, :] = jnp.where(i < nt - 1, norm(xn_ref[0]), 0.0).astype(MM)
    for r in range(0, tm, rb):
        hext_ref[HALO + r:HALO + r + rb, :] = norm(xc_ref[0, r:r + rb, :]).astype(MM)

    hcur = hext_ref[HALO:HALO + tm, :]
    z_ref[0] = jnp.dot(hcur, wz_ref[0], preferred_element_type=F32)
    ab = jnp.dot(hcur, wab_ref[0], preferred_element_type=F32)
    g = -jnp.exp(alog_ref[0]) * _softplus(ab + dt_ref[0])
    col = lax.broadcasted_iota(jnp.int32, ab.shape, 1)
    gb_ref[0] = jnp.where(col < 2 * N_HEADS, g, _sigmoid(ab))

    pext_ref[:, 0:3 * D_A] = jnp.dot(hext_ref[...], wqkv_ref[0], preferred_element_type=F32)
    glu_ref[...] = jnp.dot(hext_ref[...], wglu_ref[0], preferred_element_type=F32)
    for r in range(0, n_ext, rb):
        rows = min(rb, n_ext - r)
        pext_ref[r:r + rows, 3 * D_A:] = (glu_ref[r:r + rows, 0:D_B]
                                          * _sigmoid(glu_ref[r:r + rows, D_B:]))

    def dwconv(c0, w_ref, wc0, width, r):
        acc = None
        for t in range(width):
            off = HALO + r + t - width // 2
            res = off % SUBLANES
            if res == 0:
                val = pext_ref[off:off + rb, c0:c0 + SH_COLS]
            else:
                val = sh_ref[res - 1, off - res:off - res + rb, :]
            term = val * w_ref[0, t:t + 1, wc0:wc0 + SH_COLS]
            acc = term if acc is None else acc + term
        return acc

    def shifted_copies(c0, width):
        used = sorted({(HALO + t - width // 2) % SUBLANES for t in range(width)} - {0})
        for s in used:
            sh_ref[s - 1, :, :] = pext_ref[s:s + n_ext - SUBLANES, c0:c0 + SH_COLS]

    for part, out_ref in enumerate((q_ref, k_ref, v_ref)):
        shifted_copies(part * D_A, SHORT_CONV)
        for r in range(0, tm, rb):
            acc = _silu(dwconv(part * D_A, cq_ref, part * D_A, SHORT_CONV, r))
            if part < 2:
                heads = []
                for h in range(N_HEADS):
                    a = acc[:, h * HEAD_D:(h + 1) * HEAD_D]
                    a = a * lax.rsqrt(jnp.sum(a * a, axis=-1, keepdims=True) + EPS)
                    heads.append(a * (HEAD_D ** -0.5) if part == 0 else a)
                acc = jnp.concatenate(heads, axis=-1)
            out_ref[0, r:r + rb, :] = acc

    shifted_copies(3 * D_A, CF_CONV)
    for r in range(0, tm, rb):
        acc = dwconv(3 * D_A, ccf_ref, 0, CF_CONV, r) + bcf_ref[0]
        mu = jnp.mean(acc, axis=-1, keepdims=True)
        cen = acc - mu
        var = jnp.mean(cen * cen, axis=-1, keepdims=True)
        y = cen * lax.rsqrt(var + EPS) * lng_ref[0] + lnb_ref[0]
        u_ref[0, r:r + rb, :] = _silu(y).astype(u_ref.dtype)


def _mixer_in(x, mod, mrow, layer, wts, *, tm):
    bsz, t, d = x.shape
    nt = t // tm
    hb = tm // HALO
    nhb = t // HALO
    names = ("g_pre_mix", "w_qkv", "w_glu", "w_z", "w_ab", "conv_qkv", "conv_cf", "b_conv_cf",
             "ln_cf_g", "ln_cf_b", "a_log", "dt_bias")
    seq = lambda c: pl.BlockSpec((1, tm, c), lambda b, i: (b, i, 0))
    return pl.pallas_call(
        functools.partial(_mixer_in_kernel, tm=tm, nt=nt),
        grid=(bsz, nt),
        in_specs=[
            pl.BlockSpec((1, HALO, d), lambda b, i: (b, jnp.maximum(i * hb - 1, 0), 0)),
            pl.BlockSpec((1, tm, d), lambda b, i: (b, i, 0)),
            pl.BlockSpec((1, HALO, d), lambda b, i: (b, jnp.minimum((i + 1) * hb, nhb - 1), 0)),
            pl.BlockSpec((1, N_MOD, d), lambda b, i: (mrow(b), 0, 0)),
        ] + [_layer_spec(wts[n], layer) for n in names],
        out_specs=[seq(D_A), seq(D_A), seq(D_A), seq(D_A), seq(LANES), seq(D_B)],
        out_shape=[
            jax.ShapeDtypeStruct((bsz, t, D_A), F32),
            jax.ShapeDtypeStruct((bsz, t, D_A), F32),
            jax.ShapeDtypeStruct((bsz, t, D_A), F32),
            jax.ShapeDtypeStruct((bsz, t, D_A), F32),
            jax.ShapeDtypeStruct((bsz, t, LANES), F32),
            jax.ShapeDtypeStruct((bsz, t, D_B), MM),
        ],
        scratch_shapes=[
            pltpu.VMEM((tm + 2 * HALO, d), MM),
            pltpu.VMEM((tm + 2 * HALO, 3 * D_A + D_B), F32),
            pltpu.VMEM((tm + 2 * HALO, 2 * D_B), F32),
            pltpu.VMEM((SUBLANES - 1, tm + 2 * HALO - SUBLANES, SH_COLS), F32),
        ],
        compiler_params=pltpu.CompilerParams(vmem_limit_bytes=VMEM_LIMIT),
        name="mixer_in",
    )(x, x, x, mod, *[wts[n] for n in names])


def _dn_masks(d):
    n = N_HEADS * CHUNK
    ii = lax.broadcasted_iota(jnp.int32, (n, n), 0)
    jj = lax.broadcasted_iota(jnp.int32, (n, n), 1)
    lo, hi = (jj, ii) if d == 0 else (ii, jj)
    return ii, jj, lo, hi


def _dn_prep(q_ref, k_ref, v_ref, gb_ref, r0, d, p, amat_ref, tinv_ref, qk_ref, rhs_ref,
             qg_ref, kd_ref, egt_ref):
    rows = slice(r0, r0 + CHUNK)
    log2_chunk = CHUNK.bit_length() - 1
    gbc = gb_ref[0, rows, :]
    ci = lax.broadcasted_iota(jnp.int32, (CHUNK, CHUNK), 0)
    cj = lax.broadcasted_iota(jnp.int32, (CHUNK, CHUNK), 1)
    tri = (ci >= cj) if d == 0 else (ci <= cj)
    gc = _dot_exact(jnp.where(tri, 1.0, 0.0), gbc)
    last = CHUNK - 1 if d == 0 else 0
    gtot = gc[last:last + 1, :]
    egc = jnp.exp(gc)
    edk = jnp.exp(gtot - gc)
    egt_ref[p] = jnp.exp(gtot)
    gct = jnp.concatenate([gc, jnp.zeros_like(gc)], axis=0).T

    def stack_cols(a, c0):
        return jnp.concatenate([a[:, c0 + h:c0 + h + 1] for h in range(N_HEADS)], axis=0)

    c0 = d * N_HEADS
    gc_col = stack_cols(gc, c0)
    be_col = stack_cols(gbc, 2 * N_HEADS + c0)
    egc_col = stack_cols(egc, c0)
    edk_col = stack_cols(edk, c0)
    gc_row = jnp.concatenate([gct[c0 + h:c0 + h + 1, 0:CHUNK] for h in range(N_HEADS)], axis=1)

    def stack_heads(ref):
        return jnp.concatenate(
            [ref[0, rows, h * HEAD_D:(h + 1) * HEAD_D] for h in range(N_HEADS)], axis=0)

    qs, ks, vs = stack_heads(q_ref), stack_heads(k_ref), stack_heads(v_ref)
    ii, jj, lo, hi = _dn_masks(d)
    same = (ii >> log2_chunk) == (jj >> log2_chunk)
    incl = jnp.logical_and(same, lo <= hi)
    decay = jnp.where(incl, jnp.exp(jnp.where(incl, gc_col - gc_row, 0.0)), 0.0)
    kb = ks * be_col
    amat = _dot_nt(kb, ks) * decay
    amat_ref[p] = amat
    pair = jnp.logical_and((lo >> 1) == (hi >> 1), lo < hi)
    tinv_ref[p] = jnp.where(ii == jj, 1.0, 0.0) - jnp.where(pair, amat, 0.0)
    qk_ref[p] = (_dot_nt(qs, ks) * decay).astype(MM)
    rhs_ref[p] = jnp.concatenate([vs * be_col, kb * egc_col], axis=1).astype(MM)
    qg_ref[p] = (qs * egc_col).astype(MM)
    kd_ref[p] = (ks * edk_col).astype(MM)


def _deltanet_kernel(*refs, nc, zero_init):
    if zero_init:
        qf, kf, vf, gf, qb, kb, vb, gb, of_ref, ob_ref, sfin_ref = refs[:11]
    else:
        qf, kf, vf, gf, qb, kb, vb, gb, s0_ref, of_ref, ob_ref, sfin_ref = refs[:12]
    amat_ref, tinv_ref, qk_ref, rhs_ref, qg_ref, kd_ref, egt_ref, uw_ref, s_ref = refs[-9:]
    j = pl.program_id(1)
    log2_chunk = CHUNK.bit_length() - 1
    n = N_HEADS * CHUNK

    @pl.when(j == 0)
    def _():
        if zero_init:
            s_ref[...] = jnp.zeros_like(s_ref)
        else:
            s_ref[...] = s0_ref[0]

    probs = []
    for c in range(nc):
        probs.append((0, c * CHUNK, (qf, kf, vf, gf), of_ref))
        probs.append((1, (nc - 1 - c) * CHUNK, (qb, kb, vb, gb), ob_ref))

    for p, (d, r0, (q_ref, k_ref, v_ref, g_ref), _) in enumerate(probs):
        _dn_prep(q_ref, k_ref, v_ref, g_ref, r0, d, p, amat_ref, tinv_ref, qk_ref, rhs_ref,
                 qg_ref, kd_ref, egt_ref)

    for lb in range(1, log2_chunk):
        e_masks = []
        for d in (0, 1):
            _, _, lo, hi = _dn_masks(d)
            e_masks.append(jnp.logical_and((lo >> (lb + 1)) == (hi >> (lb + 1)),
                                           (lo >> lb) < (hi >> lb)))
        b = 1 << lb
        for p, (d, _, _, _) in enumerate(probs):
            tb = tinv_ref[p].astype(MM)
            e = jnp.where(e_masks[d], amat_ref[p], 0.0)
            if b < SUBLANES:
                tinv_ref[p] = tinv_ref[p] - _dot(_dot(tb, e), tb)
            else:
                starts = [s + (b if d == 0 else 0) for s in range(0, n, 2 * b)]
                sel = jnp.concatenate([tinv_ref[p, s:s + b, :] for s in starts], axis=0)
                x = _dot(_dot(sel, e), tb)
                for idx, s in enumerate(starts):
                    tinv_ref[p, s:s + b, :] = sel[idx * b:(idx + 1) * b] - x[idx * b:(idx + 1) * b]

    for p in range(len(probs)):
        uw_ref[p] = _dot(tinv_ref[p], rhs_ref[p])

    for p, (d, r0, _, o_ref) in enumerate(probs):
        c0 = d * N_HEADS
        vnew, o1 = [], []
        for h in range(N_HEADS):
            hs = slice(h * CHUNK, (h + 1) * CHUNK)
            wq = jnp.concatenate([uw_ref[p, hs, HEAD_D:].astype(MM), qg_ref[p, hs, :]], axis=0)
            r = _dot(wq, s_ref[d, h])
            vnew.append(uw_ref[p, hs, 0:HEAD_D] - r[:CHUNK])
            o1.append(r[CHUNK:])
        vnew = jnp.concatenate(vnew, axis=0).astype(MM)
        o = jnp.concatenate(o1, axis=0) + _dot(qk_ref[p], vnew)
        for h in range(N_HEADS):
            hs = slice(h * CHUNK, (h + 1) * CHUNK)
            s_ref[d, h] = (s_ref[d, h] * egt_ref[p, :, c0 + h:c0 + h + 1]
                           + _dot_tn(kd_ref[p, hs, :], vnew[hs]))
            o_ref[0, r0:r0 + CHUNK, h * HEAD_D:(h + 1) * HEAD_D] = o[hs]

    @pl.when(j == pl.num_programs(1) - 1)
    def _():
        sfin_ref[0] = s_ref[...]


def _deltanet(q, k, v, gb, s0, *, tt):
    bsz, t, _ = q.shape
    nt = t // tt
    nc = tt // CHUNK
    n = N_HEADS * CHUNK
    zero_init = s0 is None
    fwd = lambda c: pl.BlockSpec((1, tt, c), lambda b, j: (b, j, 0))
    bwd = lambda c: pl.BlockSpec((1, tt, c), lambda b, j: (b, nt - 1 - j, 0))
    st_spec = pl.BlockSpec((1, 2, N_HEADS, HEAD_D, HEAD_D), lambda b, j: (b, 0, 0, 0, 0))
    in_specs = [fwd(D_A), fwd(D_A), fwd(D_A), fwd(LANES), bwd(D_A), bwd(D_A), bwd(D_A), bwd(LANES)]
    args = [q, k, v, gb, q, k, v, gb]
    if not zero_init:
        in_specs.append(st_spec)
        args.append(s0)
    n_prob = 2 * nc
    return pl.pallas_call(
        functools.partial(_deltanet_kernel, nc=nc, zero_init=zero_init),
        grid=(bsz, nt),
        in_specs=in_specs,
        out_specs=[fwd(D_A), bwd(D_A), st_spec],
        out_shape=[
            jax.ShapeDtypeStruct((bsz, t, D_A), F32),
            jax.ShapeDtypeStruct((bsz, t, D_A), F32),
            jax.ShapeDtypeStruct((bsz, 2, N_HEADS, HEAD_D, HEAD_D), F32),
        ],
        scratch_shapes=[
            pltpu.VMEM((n_prob, n, n), F32),
            pltpu.VMEM((n_prob, n, n), F32),
            pltpu.VMEM((n_prob, n, n), MM),
            pltpu.VMEM((n_prob, n, 2 * HEAD_D), MM),
            pltpu.VMEM((n_prob, n, HEAD_D), MM),
            pltpu.VMEM((n_prob, n, HEAD_D), MM),
            pltpu.VMEM((n_prob, 1, LANES), F32),
            pltpu.VMEM((n_prob, n, 2 * HEAD_D), F32),
            pltpu.VMEM((2, N_HEADS, HEAD_D, HEAD_D), F32),
        ],
        compiler_params=pltpu.CompilerParams(
            dimension_semantics=("parallel", "arbitrary"), vmem_limit_bytes=VMEM_LIMIT),
        name="deltanet",
    )(*args)


def _mixer_out_kernel(of_ref, ob_ref, z_ref, u_ref, x_ref, mod_ref, gon_ref, wo_ref, gpost_ref,
                      out_ref, cat_ref, mix_ref, *, tm):
    rb = ROWS
    gate = mod_ref[0, 2:3, :]
    for r in range(0, tm, rb):
        rows = slice(r, r + rb)
        o = of_ref[0, rows, :] + ob_ref[0, rows, :]
        z = z_ref[0, rows, :]
        for h in range(N_HEADS):
            hs = slice(h * HEAD_D, (h + 1) * HEAD_D)
            oh = o[:, hs]
            oh = oh * _rms_scale(oh) * gon_ref[0]
            cat_ref[rows, hs] = (oh * _silu(z[:, hs])).astype(MM)
        cat_ref[rows, D_A:] = u_ref[0, rows, :]
    mix_ref[...] = jnp.dot(cat_ref[...], wo_ref[0], preferred_element_type=F32)
    for r in range(0, tm, rb):
        rows = slice(r, r + rb)
        mix = mix_ref[rows, :]
        mix = mix * _rms_scale(mix) * gpost_ref[0]
        out_ref[0, rows, :] = x_ref[0, rows, :] + gate * mix


def _mixer_out(o_f, o_b, z, u, x, mod, mrow, layer, wts, *, tm):
    bsz, t, d = x.shape
    names = ("g_onorm", "w_out", "g_post_mix")
    seq = lambda c: pl.BlockSpec((1, tm, c), lambda b, i: (b, i, 0))
    return pl.pallas_call(
        functools.partial(_mixer_out_kernel, tm=tm),
        grid=(bsz, t // tm),
        in_specs=[
            seq(D_A), seq(D_A), seq(D_A), seq(D_B), seq(d),
            pl.BlockSpec((1, N_MOD, d), lambda b, i: (mrow(b), 0, 0)),
        ] + [_layer_spec(wts[n], layer) for n in names],
        out_specs=seq(d),
        out_shape=jax.ShapeDtypeStruct((bsz, t, d), F32),
        scratch_shapes=[pltpu.VMEM((tm, D_A + D_B), MM), pltpu.VMEM((tm, d), F32)],
        compiler_params=pltpu.CompilerParams(vmem_limit_bytes=VMEM_LIMIT),
        name="mixer_out",
    )(o_f, o_b, z, u, x, mod, *[wts[n] for n in names])


def _ffn_kernel(xp_ref, xc_ref, xn_ref, mod_ref, gpre_ref, wg_ref, wu_ref, wd_ref, cw_ref,
                cb_ref, gpost_ref, out_ref, hext_ref, gts0_ref, gts1_ref, up0_ref, up1_ref,
                act0_ref, act1_ref, y_ref, *, tm, nt, halo, width, taps, n_chunks):
    i = pl.program_id(1)
    pad = 8
    shift = mod_ref[0, 3:4, :]
    scale1 = 1.0 + mod_ref[0, 4:5, :]
    gate = mod_ref[0, 5:6, :]
    gpre = gpre_ref[0]
    rb = ROWS
    n_ext = tm + 2 * halo

    def norm(x):
        return ((x * _rms_scale(x)) * gpre) * scale1 + shift

    if halo:
        hext_ref[0:halo, :] = jnp.where(i > 0, norm(xp_ref[0]), 0.0).astype(MM)
        hext_ref[halo + tm:n_ext, :] = jnp.where(i < nt - 1, norm(xn_ref[0]), 0.0).astype(MM)
    for r in range(0, tm, rb):
        hext_ref[halo + r:halo + r + rb, :] = norm(xc_ref[0, r:r + rb, :]).astype(MM)

    gts, ups, acts = (gts0_ref, gts1_ref), (up0_ref, up1_ref), (act0_ref, act1_ref)
    for g_ref in gts:
        g_ref[0:pad, :] = jnp.zeros((pad, FF_CHUNK), F32)
        g_ref[pad + n_ext:pad + n_ext + pad, :] = jnp.zeros((pad, FF_CHUNK), F32)
    y_ref[...] = jnp.zeros_like(y_ref)

    def project(c, slot):
        gts[slot][pad:pad + n_ext, :] = jnp.dot(hext_ref[...], wg_ref[0, c],
                                                preferred_element_type=F32)
        ups[slot][...] = jnp.dot(hext_ref[halo:halo + tm, :], wu_ref[0, c],
                                 preferred_element_type=F32)

    def activate(c, slot, rb=ACT_ROWS):
        for r in range(0, tm, rb):
            pos = (r + lax.broadcasted_iota(jnp.int32, (rb, 1), 0)) % width
            acc = None
            for dr, dc, widx in taps:
                s = pad + halo + r + dr * GRID_W + dc
                val = gts[slot][s:s + rb, :]
                if dc != 0:
                    ok = (pos >= 1) if dc < 0 else (pos <= width - 2)
                    val = jnp.where(ok, val, 0.0)
                term = val * cw_ref[0, c, widx:widx + 1, :]
                acc = term if acc is None else acc + term
            gt = acc + cb_ref[0, c]
            acts[slot][r:r + rb, :] = (_silu(gt) * ups[slot][r:r + rb, :]).astype(MM)

    def contract(c, slot):
        y_ref[...] += jnp.dot(acts[slot][...], wd_ref[0, c], preferred_element_type=F32)

    def steady(c, slot):
        contract(c - 1, 1 - slot)
        project(c + 1, 1 - slot)
        activate(c, slot)

    project(0, 0)
    project(1, 1)
    activate(0, 0)
    n_steady = n_chunks - 2

    def pair_body(k, carry):
        steady(2 * k + 1, 1)
        steady(2 * k + 2, 0)
        return carry

    lax.fori_loop(0, n_steady // 2, pair_body, 0)
    if n_steady % 2:
        steady(n_chunks - 2, (n_chunks - 2) % 2)
    last = n_chunks - 1
    contract(last - 1, (last - 1) % 2)
    activate(last, last % 2)
    contract(last, last % 2)

    for r in range(0, tm, rb):
        y = y_ref[r:r + rb, :]
        y = y * _rms_scale(y) * gpost_ref[0]
        out_ref[0, r:r + rb, :] = xc_ref[0, r:r + rb, :] + gate * y


def _ffn(x, mod, mrow, layer, wts, *, tm, grid_conv):
    bsz, t, d = x.shape
    nt = t // tm
    n_chunks = wts["w_gate"].shape[1]
    if grid_conv:
        halo = GRID_W
        width = GRID_W
        taps = tuple((dr, dc, (dr + 1) * 3 + (dc + 1)) for dr in (-1, 0, 1) for dc in (-1, 0, 1))
        hb, nhb = tm // halo, t // halo
        xp_spec = pl.BlockSpec((1, halo, d), lambda b, i: (b, jnp.maximum(i * hb - 1, 0), 0))
        xn_spec = pl.BlockSpec((1, halo, d), lambda b, i: (b, jnp.minimum((i + 1) * hb, nhb - 1), 0))
    else:
        assert tm == t
        halo = 0
        width = t
        taps = tuple((0, dc, 3 + (dc + 1)) for dc in (-1, 0, 1))
        xp_spec = pl.BlockSpec((1, 8, d), lambda b, i: (b, 0, 0))
        xn_spec = pl.BlockSpec((1, 8, d), lambda b, i: (b, 0, 0))
    names = ("g_pre_ffn", "w_gate", "w_upp", "w_down", "conv_ffn", "b_conv_ffn", "g_post_ffn")
    return pl.pallas_call(
        functools.partial(_ffn_kernel, tm=tm, nt=nt, halo=halo, width=width, taps=taps,
                          n_chunks=n_chunks),
        grid=(bsz, nt),
        in_specs=[
            xp_spec,
            pl.BlockSpec((1, tm, d), lambda b, i: (b, i, 0)),
            xn_spec,
            pl.BlockSpec((1, N_MOD, d), lambda b, i: (mrow(b), 0, 0)),
        ] + [_layer_spec(wts[n], layer) for n in names],
        out_specs=pl.BlockSpec((1, tm, d), lambda b, i: (b, i, 0)),
        out_shape=jax.ShapeDtypeStruct((bsz, t, d), F32),
        scratch_shapes=[
            pltpu.VMEM((tm + 2 * halo, d), MM),
            pltpu.VMEM((tm + 2 * halo + 16, FF_CHUNK), F32),
            pltpu.VMEM((tm + 2 * halo + 16, FF_CHUNK), F32),
            pltpu.VMEM((tm, FF_CHUNK), F32),
            pltpu.VMEM((tm, FF_CHUNK), F32),
            pltpu.VMEM((tm, FF_CHUNK), MM),
            pltpu.VMEM((tm, FF_CHUNK), MM),
            pltpu.VMEM((tm, d), F32),
        ],
        compiler_params=pltpu.CompilerParams(vmem_limit_bytes=VMEM_LIMIT),
        name="ffn",
    )(x, x, x, mod, *[wts[n] for n in names])


def _prepare_weights(w_in, conv_qkv, a_log, dt_bias, g_onorm, conv_cf, b_conv_cf, ln_cf_g,
                     ln_cf_b, w_out, w_up, conv_ffn, b_conv_ffn, w_down, g_pre_mix, g_post_mix,
                     g_pre_ffn, g_post_ffn):
    depth, d, _ = w_in.shape
    d_ff = w_down.shape[1]
    n_chunks = d_ff // FF_CHUNK
    n_ab = 4 * N_HEADS
    ab = w_in[:, :, 4 * D_A:4 * D_A + n_ab]
    vec = lambda a: a.reshape(depth, 1, -1)
    pad_lane = lambda a: jnp.pad(a.reshape(depth, 1, -1),
                                 ((0, 0), (0, 0), (0, LANES - 2 * N_HEADS)))
    chunked = lambda w: jnp.transpose(w.reshape(depth, d, n_chunks, FF_CHUNK), (0, 2, 1, 3))
    return {
        "w_qkv": w_in[:, :, :3 * D_A].astype(MM),
        "w_z": w_in[:, :, 3 * D_A:4 * D_A].astype(MM),
        "w_ab": jnp.pad(ab, ((0, 0), (0, 0), (0, LANES - n_ab))).astype(MM),
        "w_glu": w_in[:, :, 4 * D_A + n_ab:].astype(MM),
        "conv_qkv": conv_qkv, "conv_cf": conv_cf, "b_conv_cf": vec(b_conv_cf),
        "ln_cf_g": vec(ln_cf_g), "ln_cf_b": vec(ln_cf_b),
        "a_log": pad_lane(a_log), "dt_bias": pad_lane(dt_bias),
        "g_onorm": vec(g_onorm), "w_out": w_out.astype(MM),
        "w_gate": chunked(w_up[:, :, :d_ff]).astype(MM),
        "w_upp": chunked(w_up[:, :, d_ff:]).astype(MM),
        "w_down": w_down.reshape(depth, n_chunks, FF_CHUNK, d).astype(MM),
        "conv_ffn": jnp.transpose(
            conv_ffn.reshape(depth, 9, n_chunks, FF_CHUNK), (0, 2, 1, 3)),
        "b_conv_ffn": b_conv_ffn.reshape(depth, n_chunks, 1, FF_CHUNK),
        "g_pre_mix": vec(g_pre_mix), "g_post_mix": vec(g_post_mix),
        "g_pre_ffn": vec(g_pre_ffn), "g_post_ffn": vec(g_post_ffn),
    }


def _trunk_layer(x, mod, mrow, s0, layer, wts, *, tm_in, tt, tm_out, tm_ffn, grid_conv):
    q, k, v, z, gb, u = _mixer_in(x, mod, mrow, layer, wts, tm=tm_in)
    o_f, o_b, s_fin = _deltanet(q, k, v, gb, s0, tt=tt)
    x = _mixer_out(o_f, o_b, z, u, x, mod, mrow, layer, wts, tm=tm_out)
    x = _ffn(x, mod, mrow, layer, wts, tm=tm_ffn, grid_conv=grid_conv)
    return x, s_fin


def kernel(x_prompt, x_sample, state_delta, c, c_ctx, w_mod, b_mod, g_pre_mix, g_post_mix,
           g_pre_ffn, g_post_ffn, w_in, conv_qkv, a_log, dt_bias, g_onorm, conv_cf, b_conv_cf,
           ln_cf_g, ln_cf_b, w_out, w_up, conv_ffn, b_conv_ffn, w_down):
    depth, d = g_pre_mix.shape
    dec_b = c.shape[0]
    assert dec_b < 8
    cc = jnp.concatenate([c, c_ctx[None, :], jnp.zeros((8 - dec_b - 1, d), c.dtype)], axis=0)
    mod = _modulation(cc, w_mod, b_mod).reshape(depth, 8, N_MOD, d)
    wts = _prepare_weights(w_in, conv_qkv, a_log, dt_bias, g_onorm, conv_cf, b_conv_cf, ln_cf_g,
                           ln_cf_b, w_out, w_up, conv_ffn, b_conv_ffn, w_down, g_pre_mix,
                           g_post_mix, g_pre_ffn, g_post_ffn)
    xp, xs = x_prompt, x_sample
    seq = xp.shape[1]
    lat_tile = min(512, xs.shape[1])
    ctx_states = []
    for layer in range(depth):
        xp, s_ctx = _trunk_layer(xp, mod[layer], lambda b: dec_b, None, layer, wts,
                                 tm_in=seq, tt=seq, tm_out=seq, tm_ffn=seq, grid_conv=False)
        ctx_states.append(s_ctx)
        xs, _ = _trunk_layer(xs, mod[layer], lambda b: b, state_delta[:, layer], layer, wts,
                             tm_in=lat_tile, tt=256, tm_out=lat_tile, tm_ffn=lat_tile,
                             grid_conv=True)
    return xp, xs, jnp.stack(ctx_states, axis=1)
```

```python
import functools

import jax
import jax.numpy as jnp
from jax import lax
from jax.experimental import pallas as pl
from jax.experimental.pallas import tpu as pltpu

F32 = jnp.float32
MM = jnp.bfloat16
EPS = 1e-6
GRID_W = 64
N_HEADS = 4
HEAD_D = 128
D_A = N_HEADS * HEAD_D
D_B = 512
SHORT_CONV = 5
CF_CONV = 31
CHUNK = 64
N_MOD = 6
HALO = 16
FF_CHUNK = 256
LANES = 128
SUBLANES = 8
SH_COLS = 512
ROWS = 64
FFN_UNROLL_MAX_ROWS = 256
ACT_ROWS = 64
VMEM_LIMIT = 56 * 1024 * 1024


def _sigmoid(x):
    return 1.0 / (1.0 + jnp.exp(-x))


def _silu(x):
    return x * _sigmoid(x)


def _softplus(x):
    return jnp.maximum(x, 0.0) + jnp.log(1.0 + jnp.exp(-jnp.abs(x)))


def _dot(a, b):
    return jnp.dot(a.astype(MM), b.astype(MM), preferred_element_type=F32)


def _dot_nt(a, b):
    return lax.dot_general(a.astype(MM), b.astype(MM), (((1,), (1,)), ((), ())),
                           preferred_element_type=F32)


def _dot_tn(a, b):
    return lax.dot_general(a.astype(MM), b.astype(MM), (((0,), (0,)), ((), ())),
                           preferred_element_type=F32)


def _dot_exact(a, b):
    return jnp.dot(a, b, preferred_element_type=F32, precision=lax.Precision.HIGHEST)


def _rms_scale(x):
    return lax.rsqrt(jnp.mean(x * x, axis=-1, keepdims=True) + EPS)


def _mod_kernel(c_ref, w_ref, b_ref, o_ref):
    o_ref[0] = _dot_exact(_silu(c_ref[...]), w_ref[0]) + b_ref[0]


def _modulation(cc, w_mod, b_mod):
    depth, d, n = w_mod.shape
    tn = 1536
    return pl.pallas_call(
        _mod_kernel,
        grid=(depth, n // tn),
        in_specs=[
            pl.BlockSpec((8, d), lambda l, j: (0, 0)),
            pl.BlockSpec((1, d, tn), lambda l, j: (l, 0, j)),
            pl.BlockSpec((1, 1, tn), lambda l, j: (l, 0, j)),
        ],
        out_specs=pl.BlockSpec((1, 8, tn), lambda l, j: (l, 0, j)),
        out_shape=jax.ShapeDtypeStruct((depth, 8, n), F32),
        compiler_params=pltpu.CompilerParams(vmem_limit_bytes=VMEM_LIMIT),
        name="modulation",
    )(cc, w_mod, b_mod.reshape(depth, 1, n))


def _layer_spec(a, layer):
    return pl.BlockSpec((1,) + a.shape[1:], lambda b, i: (layer,) + (0,) * (a.ndim - 1))


def _mixer_in_kernel(xp_ref, xc_ref, xn_ref, mod_ref, gpre_ref, wqkv_ref, wglu_ref, wz_ref,
                     wab_ref, cq_ref, ccf_ref, bcf_ref, lng_ref, lnb_ref, alog_ref, dt_ref,
                     q_ref, k_ref, v_ref, z_ref, gb_ref, u_ref,
                     hext_ref, pext_ref, glu_ref, sh_ref, *, tm, nt):
    i = pl.program_id(1)
    shift = mod_ref[0, 0:1, :]
    scale1 = 1.0 + mod_ref[0, 1:2, :]
    gpre = gpre_ref[0]
    rb = ROWS
    n_ext = tm + 2 * HALO

    def norm(x):
        return ((x * _rms_scale(x)) * gpre) * scale1 + shift

    hext_ref[0:HALO, :] = jnp.where(i > 0, norm(xp_ref[0]), 0.0).astype(MM)
    hext_ref[HALO + tm:n_ext, :] = jnp.where(i < nt - 1, norm(xn_ref[0]), 0.0).astype(MM)
    for r in range(0, tm, rb):
        hext_ref[HALO + r:HALO + r + rb, :] = norm(xc_ref[0, r:r + rb, :]).astype(MM)

    hcur = hext_ref[HALO:HALO + tm, :]
    z_ref[0] = jnp.dot(hcur, wz_ref[0], preferred_element_type=F32)
    ab = jnp.dot(hcur, wab_ref[0], preferred_element_type=F32)
    g = -jnp.exp(alog_ref[0]) * _softplus(ab + dt_ref[0])
    col = lax.broadcasted_iota(jnp.int32, ab.shape, 1)
    gb_ref[0] = jnp.where(col < 2 * N_HEADS, g, _sigmoid(ab))

    pext_ref[:, 0:3 * D_A] = jnp.dot(hext_ref[...], wqkv_ref[0], preferred_element_type=F32)
    glu_ref[...] = jnp.dot(hext_ref[...], wglu_ref[0], preferred_element_type=F32)
    for r in range(0, n_ext, rb):
        rows = min(rb, n_ext - r)
        pext_ref[r:r + rows, 3 * D_A:] = (glu_ref[r:r + rows, 0:D_B]
                                          * _sigmoid(glu_ref[r:r + rows, D_B:]))

    def dwconv(c0, w_ref, wc0, width, r):
        acc = None
        for t in range(width):
            off = HALO + r + t - width // 2
            res = off % SUBLANES
            if res == 0:
                val = pext_ref[off:off + rb, c0:c0 + SH_COLS]
            else:
                val = sh_ref[res - 1, off - res:off - res + rb, :]
            term = val * w_ref[0, t:t + 1, wc0:wc0 + SH_COLS]
            acc = term if acc is None else acc + term
        return acc

    def shifted_copies(c0, width):
        used = sorted({(HALO + t - width // 2) % SUBLANES for t in range(width)} - {0})
        for s in used:
            sh_ref[s - 1, :, :] = pext_ref[s:s + n_ext - SUBLANES, c0:c0 + SH_COLS]

    for part, out_ref in enumerate((q_ref, k_ref, v_ref)):
        shifted_copies(part * D_A, SHORT_CONV)
        for r in range(0, tm, rb):
            acc = _silu(dwconv(part * D_A, cq_ref, part * D_A, SHORT_CONV, r))
            if part < 2:
                heads = []
                for h in range(N_HEADS):
                    a = acc[:, h * HEAD_D:(h + 1) * HEAD_D]
                    a = a * lax.rsqrt(jnp.sum(a * a, axis=-1, keepdims=True) + EPS)
                    heads.append(a * (HEAD_D ** -0.5) if part == 0 else a)
                acc = jnp.concatenate(heads, axis=-1)
            out_ref[0, r:r + rb, :] = acc

    shifted_copies(3 * D_A, CF_CONV)
    for r in range(0, tm, rb):
        acc = dwconv(3 * D_A, ccf_ref, 0, CF_CONV, r) + bcf_ref[0]
        mu = jnp.mean(acc, axis=-1, keepdims=True)
        cen = acc - mu
        var = jnp.mean(cen * cen, axis=-1, keepdims=True)
        y = cen * lax.rsqrt(var + EPS) * lng_ref[0] + lnb_ref[0]
        u_ref[0, r:r + rb, :] = _silu(y).astype(u_ref.dtype)


def _mixer_in(x, mod, mrow, layer, wts, *, tm):
    bsz, t, d = x.shape
    nt = t // tm
    hb = tm // HALO
    nhb = t // HALO
    names = ("g_pre_mix", "w_qkv", "w_glu", "w_z", "w_ab", "conv_qkv", "conv_cf", "b_conv_cf",
             "ln_cf_g", "ln_cf_b", "a_log", "dt_bias")
    seq = lambda c: pl.BlockSpec((1, tm, c), lambda b, i: (b, i, 0))
    return pl.pallas_call(
        functools.partial(_mixer_in_kernel, tm=tm, nt=nt),
        grid=(bsz, nt),
        in_specs=[
            pl.BlockSpec((1, HALO, d), lambda b, i: (b, jnp.maximum(i * hb - 1, 0), 0)),
            pl.BlockSpec((1, tm, d), lambda b, i: (b, i, 0)),
            pl.BlockSpec((1, HALO, d), lambda b, i: (b, jnp.minimum((i + 1) * hb, nhb - 1), 0)),
            pl.BlockSpec((1, N_MOD, d), lambda b, i: (mrow(b), 0, 0)),
        ] + [_layer_spec(wts[n], layer) for n in names],
        out_specs=[seq(D_A), seq(D_A), seq(D_A), seq(D_A), seq(LANES), seq(D_B)],
        out_shape=[
            jax.ShapeDtypeStruct((bsz, t, D_A), F32),
            jax.ShapeDtypeStruct((bsz, t, D_A), F32),
            jax.ShapeDtypeStruct((bsz, t, D_A), F32),
            jax.ShapeDtypeStruct((bsz, t, D_A), F32),
            jax.ShapeDtypeStruct((bsz, t, LANES), F32),
            jax.ShapeDtypeStruct((bsz, t, D_B), MM),
        ],
        scratch_shapes=[
            pltpu.VMEM((tm + 2 * HALO, d), MM),
            pltpu.VMEM((tm + 2 * HALO, 3 * D_A + D_B), F32),
            pltpu.VMEM((tm + 2 * HALO, 2 * D_B), F32),
            pltpu.VMEM((SUBLANES - 1, tm + 2 * HALO - SUBLANES, SH_COLS), F32),
        ],
        compiler_params=pltpu.CompilerParams(vmem_limit_bytes=VMEM_LIMIT),
        name="mixer_in",
    )(x, x, x, mod, *[wts[n] for n in names])


def _dn_masks(d):
    n = N_HEADS * CHUNK
    ii = lax.broadcasted_iota(jnp.int32, (n, n), 0)
    jj = lax.broadcasted_iota(jnp.int32, (n, n), 1)
    lo, hi = (jj, ii) if d == 0 else (ii, jj)
    return ii, jj, lo, hi


def _dn_prep(q_ref, k_ref, v_ref, gb_ref, r0, d, p, amat_ref, tinv_ref, qk_ref, rhs_ref,
             qg_ref, kd_ref, egt_ref):
    rows = slice(r0, r0 + CHUNK)
    log2_chunk = CHUNK.bit_length() - 1
    gbc = gb_ref[0, rows, :]
    ci = lax.broadcasted_iota(jnp.int32, (CHUNK, CHUNK), 0)
    cj = lax.broadcasted_iota(jnp.int32, (CHUNK, CHUNK), 1)
    tri = (ci >= cj) if d == 0 else (ci <= cj)
    gc = _dot_exact(jnp.where(tri, 1.0, 0.0), gbc)
    last = CHUNK - 1 if d == 0 else 0
    gtot = gc[last:last + 1, :]
    egc = jnp.exp(gc)
    edk = jnp.exp(gtot - gc)
    egt_ref[p] = jnp.exp(gtot)
    gct = jnp.concatenate([gc, jnp.zeros_like(gc)], axis=0).T

    def stack_cols(a, c0):
        return jnp.concatenate([a[:, c0 + h:c0 + h + 1] for h in range(N_HEADS)], axis=0)

    c0 = d * N_HEADS
    gc_col = stack_cols(gc, c0)
    be_col = stack_cols(gbc, 2 * N_HEADS + c0)
    egc_col = stack_cols(egc, c0)
    edk_col = stack_cols(edk, c0)
    gc_row = jnp.concatenate([gct[c0 + h:c0 + h + 1, 0:CHUNK] for h in range(N_HEADS)], axis=1)

    def stack_heads(ref):
        return jnp.concatenate(
            [ref[0, rows, h * HEAD_D:(h + 1) * HEAD_D] for h in range(N_HEADS)], axis=0)

    qs, ks, vs = stack_heads(q_ref), stack_heads(k_ref), stack_heads(v_ref)
    ii, jj, lo, hi = _dn_masks(d)
    same = (ii >> log2_chunk) == (jj >> log2_chunk)
    incl = jnp.logical_and(same, lo <= hi)
    decay = jnp.where(incl, jnp.exp(jnp.where(incl, gc_col - gc_row, 0.0)), 0.0)
    kb = ks * be_col
    amat = _dot_nt(kb, ks) * decay
    amat_ref[p] = amat
    pair = jnp.logical_and((lo >> 1) == (hi >> 1), lo < hi)
    tinv_ref[p] = jnp.where(ii == jj, 1.0, 0.0) - jnp.where(pair, amat, 0.0)
    qk_ref[p] = (_dot_nt(qs, ks) * decay).astype(MM)
    rhs_ref[p] = jnp.concatenate([vs * be_col, kb * egc_col], axis=1).astype(MM)
    qg_ref[p] = (qs * egc_col).astype(MM)
    kd_ref[p] = (ks * edk_col).astype(MM)


def _deltanet_kernel(*refs, nc, zero_init):
    if zero_init:
        qf, kf, vf, gf, qb, kb, vb, gb, of_ref, ob_ref, sfin_ref = refs[:11]
    else:
        qf, kf, vf, gf, qb, kb, vb, gb, s0_ref, of_ref, ob_ref, sfin_ref = refs[:12]
    amat_ref, tinv_ref, qk_ref, rhs_ref, qg_ref, kd_ref, egt_ref, uw_ref, s_ref = refs[-9:]
    j = pl.program_id(1)
    log2_chunk = CHUNK.bit_length() - 1
    n = N_HEADS * CHUNK

    @pl.when(j == 0)
    def _():
        if zero_init:
            s_ref[...] = jnp.zeros_like(s_ref)
        else:
            s_ref[...] = s0_ref[0]

    probs = []
    for c in range(nc):
        probs.append((0, c * CHUNK, (qf, kf, vf, gf), of_ref))
        probs.append((1, (nc - 1 - c) * CHUNK, (qb, kb, vb, gb), ob_ref))

    for p, (d, r0, (q_ref, k_ref, v_ref, g_ref), _) in enumerate(probs):
        _dn_prep(q_ref, k_ref, v_ref, g_ref, r0, d, p, amat_ref, tinv_ref, qk_ref, rhs_ref,
                 qg_ref, kd_ref, egt_ref)

    for lb in range(1, log2_chunk):
        e_masks = []
        for d in (0, 1):
            _, _, lo, hi = _dn_masks(d)
            e_masks.append(jnp.logical_and((lo >> (lb + 1)) == (hi >> (lb + 1)),
                                           (lo >> lb) < (hi >> lb)))
        b = 1 << lb
        for p, (d, _, _, _) in enumerate(probs):
            tb = tinv_ref[p].astype(MM)
            e = jnp.where(e_masks[d], amat_ref[p], 0.0)
            if b < SUBLANES:
                tinv_ref[p] = tinv_ref[p] - _dot(_dot(tb, e), tb)
            else:
                starts = [s + (b if d == 0 else 0) for s in range(0, n, 2 * b)]
                sel = jnp.concatenate([tinv_ref[p, s:s + b, :] for s in starts], axis=0)
                x = _dot(_dot(sel, e), tb)
                for idx, s in enumerate(starts):
                    tinv_ref[p, s:s + b, :] = sel[idx * b:(idx + 1) * b] - x[idx * b:(idx + 1) * b]

    for p in range(len(probs)):
        uw_ref[p] = _dot(tinv_ref[p], rhs_ref[p])

    for p, (d, r0, _, o_ref) in enumerate(probs):
        c0 = d * N_HEADS
        vnew, o1 = [], []
        for h in range(N_HEADS):
            hs = slice(h * CHUNK, (h + 1) * CHUNK)
            wq = jnp.concatenate([uw_ref[p, hs, HEAD_D:].astype(MM), qg_ref[p, hs, :]], axis=0)
            r = _dot(wq, s_ref[d, h])
            vnew.append(uw_ref[p, hs, 0:HEAD_D] - r[:CHUNK])
            o1.append(r[CHUNK:])
        vnew = jnp.concatenate(vnew, axis=0).astype(MM)
        o = jnp.concatenate(o1, axis=0) + _dot(qk_ref[p], vnew)
        for h in range(N_HEADS):
            hs = slice(h * CHUNK, (h + 1) * CHUNK)
            s_ref[d, h] = (s_ref[d, h] * egt_ref[p, :, c0 + h:c0 + h + 1]
                           + _dot_tn(kd_ref[p, hs, :], vnew[hs]))
            o_ref[0, r0:r0 + CHUNK, h * HEAD_D:(h + 1) * HEAD_D] = o[hs]

    @pl.when(j == pl.num_programs(1) - 1)
    def _():
        sfin_ref[0] = s_ref[...]


def _deltanet(q, k, v, gb, s0, *, tt):
    bsz, t, _ = q.shape
    nt = t // tt
    nc = tt // CHUNK
    n = N_HEADS * CHUNK
    zero_init = s0 is None
    fwd = lambda c: pl.BlockSpec((1, tt, c), lambda b, j: (b, j, 0))
    bwd = lambda c: pl.BlockSpec((1, tt, c), lambda b, j: (b, nt - 1 - j, 0))
    st_spec = pl.BlockSpec((1, 2, N_HEADS, HEAD_D, HEAD_D), lambda b, j: (b, 0, 0, 0, 0))
    in_specs = [fwd(D_A), fwd(D_A), fwd(D_A), fwd(LANES), bwd(D_A), bwd(D_A), bwd(D_A), bwd(LANES)]
    args = [q, k, v, gb, q, k, v, gb]
    if not zero_init:
        in_specs.append(st_spec)
        args.append(s0)
    n_prob = 2 * nc
    return pl.pallas_call(
        functools.partial(_deltanet_kernel, nc=nc, zero_init=zero_init),
        grid=(bsz, nt),
        in_specs=in_specs,
        out_specs=[fwd(D_A), bwd(D_A), st_spec],
        out_shape=[
            jax.ShapeDtypeStruct((bsz, t, D_A), F32),
            jax.ShapeDtypeStruct((bsz, t, D_A), F32),
            jax.ShapeDtypeStruct((bsz, 2, N_HEADS, HEAD_D, HEAD_D), F32),
        ],
        scratch_shapes=[
            pltpu.VMEM((n_prob, n, n), F32),
            pltpu.VMEM((n_prob, n, n), F32),
            pltpu.VMEM((n_prob, n, n), MM),
            pltpu.VMEM((n_prob, n, 2 * HEAD_D), MM),
            pltpu.VMEM((n_prob, n, HEAD_D), MM),
            pltpu.VMEM((n_prob, n, HEAD_D), MM),
            pltpu.VMEM((n_prob, 1, LANES), F32),
            pltpu.VMEM((n_prob, n, 2 * HEAD_D), F32),
            pltpu.VMEM((2, N_HEADS, HEAD_D, HEAD_D), F32),
        ],
        compiler_params=pltpu.CompilerParams(
            dimension_semantics=("parallel", "arbitrary"), vmem_limit_bytes=VMEM_LIMIT),
        name="deltanet",
    )(*args)


def _mixer_out_kernel(of_ref, ob_ref, z_ref, u_ref, x_ref, mod_ref, gon_ref, wo_ref, gpost_ref,
                      out_ref, cat_ref, mix_ref, *, tm):
    rb = ROWS
    gate = mod_ref[0, 2:3, :]
    for r in range(0, tm, rb):
        rows = slice(r, r + rb)
        o = of_ref[0, rows, :] + ob_ref[0, rows, :]
        z = z_ref[0, rows, :]
        for h in range(N_HEADS):
            hs = slice(h * HEAD_D, (h + 1) * HEAD_D)
            oh = o[:, hs]
            oh = oh * _rms_scale(oh) * gon_ref[0]
            cat_ref[rows, hs] = (oh * _silu(z[:, hs])).astype(MM)
        cat_ref[rows, D_A:] = u_ref[0, rows, :]
    mix_ref[...] = jnp.dot(cat_ref[...], wo_ref[0], preferred_element_type=F32)
    for r in range(0, tm, rb):
        rows = slice(r, r + rb)
        mix = mix_ref[rows, :]
        mix = mix * _rms_scale(mix) * gpost_ref[0]
        out_ref[0, rows, :] = x_ref[0, rows, :] + gate * mix


def _mixer_out(o_f, o_b, z, u, x, mod, mrow, layer, wts, *, tm):
    bsz, t, d = x.shape
    names = ("g_onorm", "w_out", "g_post_mix")
    seq = lambda c: pl.BlockSpec((1, tm, c), lambda b, i: (b, i, 0))
    return pl.pallas_call(
        functools.partial(_mixer_out_kernel, tm=tm),
        grid=(bsz, t // tm),
        in_specs=[
            seq(D_A), seq(D_A), seq(D_A), seq(D_B), seq(d),
            pl.BlockSpec((1, N_MOD, d), lambda b, i: (mrow(b), 0, 0)),
        ] + [_layer_spec(wts[n], layer) for n in names],
        out_specs=seq(d),
        out_shape=jax.ShapeDtypeStruct((bsz, t, d), F32),
        scratch_shapes=[pltpu.VMEM((tm, D_A + D_B), MM), pltpu.VMEM((tm, d), F32)],
        compiler_params=pltpu.CompilerParams(vmem_limit_bytes=VMEM_LIMIT),
        name="mixer_out",
    )(o_f, o_b, z, u, x, mod, *[wts[n] for n in names])


def _ffn_kernel(xp_ref, xc_ref, xn_ref, mod_ref, gpre_ref, wg_ref, wu_ref, wd_ref, cw_ref,
                cb_ref, gpost_ref, out_ref, hext_ref, gts0_ref, gts1_ref, up0_ref, up1_ref,
                act0_ref, act1_ref, y_ref, *, tm, nt, halo, width, taps, n_chunks, unroll_all):
    i = pl.program_id(1)
    pad = 8
    shift = mod_ref[0, 3:4, :]
    scale1 = 1.0 + mod_ref[0, 4:5, :]
    gate = mod_ref[0, 5:6, :]
    gpre = gpre_ref[0]
    rb = ROWS
    n_ext = tm + 2 * halo

    def norm(x):
        return ((x * _rms_scale(x)) * gpre) * scale1 + shift

    if halo:
        hext_ref[0:halo, :] = jnp.where(i > 0, norm(xp_ref[0]), 0.0).astype(MM)
        hext_ref[halo + tm:n_ext, :] = jnp.where(i < nt - 1, norm(xn_ref[0]), 0.0).astype(MM)
    for r in range(0, tm, rb):
        hext_ref[halo + r:halo + r + rb, :] = norm(xc_ref[0, r:r + rb, :]).astype(MM)

    gts, ups, acts = (gts0_ref, gts1_ref), (up0_ref, up1_ref), (act0_ref, act1_ref)
    for g_ref in gts:
        g_ref[0:pad, :] = jnp.zeros((pad, FF_CHUNK), F32)
        g_ref[pad + n_ext:pad + n_ext + pad, :] = jnp.zeros((pad, FF_CHUNK), F32)
    y_ref[...] = jnp.zeros_like(y_ref)

    def project(c, slot):
        gts[slot][pad:pad + n_ext, :] = jnp.dot(hext_ref[...], wg_ref[0, c],
                                                preferred_element_type=F32)
        ups[slot][...] = jnp.dot(hext_ref[halo:halo + tm, :], wu_ref[0, c],
                                 preferred_element_type=F32)

    def activate(c, slot, rb=ACT_ROWS):
        for r in range(0, tm, rb):
            pos = (r + lax.broadcasted_iota(jnp.int32, (rb, 1), 0)) % width
            acc = None
            for dr, dc, widx in taps:
                s = pad + halo + r + dr * GRID_W + dc
                val = gts[slot][s:s + rb, :]
                if dc != 0:
                    ok = (pos >= 1) if dc < 0 else (pos <= width - 2)
                    val = jnp.where(ok, val, 0.0)
                term = val * cw_ref[0, c, widx:widx + 1, :]
                acc = term if acc is None else acc + term
            gt = acc + cb_ref[0, c]
            acts[slot][r:r + rb, :] = (_silu(gt) * ups[slot][r:r + rb, :]).astype(MM)

    def contract(c, slot):
        y_ref[...] += jnp.dot(acts[slot][...], wd_ref[0, c], preferred_element_type=F32)

    def steady(c, slot):
        activate(c, slot)
        project(c + 1, 1 - slot)
        contract(c, slot)

    project(0, 0)
    n_steady = n_chunks - 1
    if unroll_all:
        n_looped = 0
    else:
        n_looped = n_steady - n_steady % 2

        def pair_body(k, carry):
            steady(2 * k, 0)
            steady(2 * k + 1, 1)
            return carry

        lax.fori_loop(0, n_looped // 2, pair_body, 0)
    for c in range(n_looped, n_steady):
        steady(c, c % 2)
    last = n_chunks - 1
    activate(last, last % 2)
    contract(last, last % 2)

    for r in range(0, tm, rb):
        y = y_ref[r:r + rb, :]
        y = y * _rms_scale(y) * gpost_ref[0]
        out_ref[0, r:r + rb, :] = xc_ref[0, r:r + rb, :] + gate * y


def _ffn(x, mod, mrow, layer, wts, *, tm, grid_conv):
    bsz, t, d = x.shape
    nt = t // tm
    n_chunks = wts["w_gate"].shape[1]
    if grid_conv:
        halo = GRID_W
        width = GRID_W
        taps = tuple((dr, dc, (dr + 1) * 3 + (dc + 1)) for dr in (-1, 0, 1) for dc in (-1, 0, 1))
        hb, nhb = tm // halo, t // halo
        xp_spec = pl.BlockSpec((1, halo, d), lambda b, i: (b, jnp.maximum(i * hb - 1, 0), 0))
        xn_spec = pl.BlockSpec((1, halo, d), lambda b, i: (b, jnp.minimum((i + 1) * hb, nhb - 1), 0))
    else:
        assert tm == t
        halo = 0
        width = t
        taps = tuple((0, dc, 3 + (dc + 1)) for dc in (-1, 0, 1))
        xp_spec = pl.BlockSpec((1, 8, d), lambda b, i: (b, 0, 0))
        xn_spec = pl.BlockSpec((1, 8, d), lambda b, i: (b, 0, 0))
    names = ("g_pre_ffn", "w_gate", "w_upp", "w_down", "conv_ffn", "b_conv_ffn", "g_post_ffn")
    return pl.pallas_call(
        functools.partial(_ffn_kernel, tm=tm, nt=nt, halo=halo, width=width, taps=taps,
                          n_chunks=n_chunks, unroll_all=tm <= FFN_UNROLL_MAX_ROWS),
        grid=(bsz, nt),
        in_specs=[
            xp_spec,
            pl.BlockSpec((1, tm, d), lambda b, i: (b, i, 0)),
            xn_spec,
            pl.BlockSpec((1, N_MOD, d), lambda b, i: (mrow(b), 0, 0)),
        ] + [_layer_spec(wts[n], layer) for n in names],
        out_specs=pl.BlockSpec((1, tm, d), lambda b, i: (b, i, 0)),
        out_shape=jax.ShapeDtypeStruct((bsz, t, d), F32),
        scratch_shapes=[
            pltpu.VMEM((tm + 2 * halo, d), MM),
            pltpu.VMEM((tm + 2 * halo + 16, FF_CHUNK), F32),
            pltpu.VMEM((tm + 2 * halo + 16, FF_CHUNK), F32),
            pltpu.VMEM((tm, FF_CHUNK), F32),
            pltpu.VMEM((tm, FF_CHUNK), F32),
            pltpu.VMEM((tm, FF_CHUNK), MM),
            pltpu.VMEM((tm, FF_CHUNK), MM),
            pltpu.VMEM((tm, d), F32),
        ],
        compiler_params=pltpu.CompilerParams(vmem_limit_bytes=VMEM_LIMIT),
        name="ffn",
    )(x, x, x, mod, *[wts[n] for n in names])


def _prepare_weights(w_in, conv_qkv, a_log, dt_bias, g_onorm, conv_cf, b_conv_cf, ln_cf_g,
                     ln_cf_b, w_out, w_up, conv_ffn, b_conv_ffn, w_down, g_pre_mix, g_post_mix,
                     g_pre_ffn, g_post_ffn):
    depth, d, _ = w_in.shape
    d_ff = w_down.shape[1]
    n_chunks = d_ff // FF_CHUNK
    n_ab = 4 * N_HEADS
    ab = w_in[:, :, 4 * D_A:4 * D_A + n_ab]
    vec = lambda a: a.reshape(depth, 1, -1)
    pad_lane = lambda a: jnp.pad(a.reshape(depth, 1, -1),
                                 ((0, 0), (0, 0), (0, LANES - 2 * N_HEADS)))
    chunked = lambda w: jnp.transpose(w.reshape(depth, d, n_chunks, FF_CHUNK), (0, 2, 1, 3))
    return {
        "w_qkv": w_in[:, :, :3 * D_A].astype(MM),
        "w_z": w_in[:, :, 3 * D_A:4 * D_A].astype(MM),
        "w_ab": jnp.pad(ab, ((0, 0), (0, 0), (0, LANES - n_ab))).astype(MM),
        "w_glu": w_in[:, :, 4 * D_A + n_ab:].astype(MM),
        "conv_qkv": conv_qkv, "conv_cf": conv_cf, "b_conv_cf": vec(b_conv_cf),
        "ln_cf_g": vec(ln_cf_g), "ln_cf_b": vec(ln_cf_b),
        "a_log": pad_lane(a_log), "dt_bias": pad_lane(dt_bias),
        "g_onorm": vec(g_onorm), "w_out": w_out.astype(MM),
        "w_gate": chunked(w_up[:, :, :d_ff]).astype(MM),
        "w_upp": chunked(w_up[:, :, d_ff:]).astype(MM),
        "w_down": w_down.reshape(depth, n_chunks, FF_CHUNK, d).astype(MM),
        "conv_ffn": jnp.transpose(
            conv_ffn.reshape(depth, 9, n_chunks, FF_CHUNK), (0, 2, 1, 3)),
        "b_conv_ffn": b_conv_ffn.reshape(depth, n_chunks, 1, FF_CHUNK),
        "g_pre_mix": vec(g_pre_mix), "g_post_mix": vec(g_post_mix),
        "g_pre_ffn": vec(g_pre_ffn), "g_post_ffn": vec(g_post_ffn),
    }


def _trunk_layer(x, mod, mrow, s0, layer, wts, *, tm_in, tt, tm_out, tm_ffn, grid_conv):
    q, k, v, z, gb, u = _mixer_in(x, mod, mrow, layer, wts, tm=tm_in)
    o_f, o_b, s_fin = _deltanet(q, k, v, gb, s0, tt=tt)
    x = _mixer_out(o_f, o_b, z, u, x, mod, mrow, layer, wts, tm=tm_out)
    x = _ffn(x, mod, mrow, layer, wts, tm=tm_ffn, grid_conv=grid_conv)
    return x, s_fin


def kernel(x_prompt, x_sample, state_delta, c, c_ctx, w_mod, b_mod, g_pre_mix, g_post_mix,
           g_pre_ffn, g_post_ffn, w_in, conv_qkv, a_log, dt_bias, g_onorm, conv_cf, b_conv_cf,
           ln_cf_g, ln_cf_b, w_out, w_up, conv_ffn, b_conv_ffn, w_down):
    depth, d = g_pre_mix.shape
    dec_b = c.shape[0]
    assert dec_b < 8
    cc = jnp.concatenate([c, c_ctx[None, :], jnp.zeros((8 - dec_b - 1, d), c.dtype)], axis=0)
    mod = _modulation(cc, w_mod, b_mod).reshape(depth, 8, N_MOD, d)
    wts = _prepare_weights(w_in, conv_qkv, a_log, dt_bias, g_onorm, conv_cf, b_conv_cf, ln_cf_g,
                           ln_cf_b, w_out, w_up, conv_ffn, b_conv_ffn, w_down, g_pre_mix,
                           g_post_mix, g_pre_ffn, g_post_ffn)
    xp, xs = x_prompt, x_sample
    seq = xp.shape[1]
    lat_tile = min(512, xs.shape[1])
    ctx_states = []
    for layer in range(depth):
        xp, s_ctx = _trunk_layer(xp, mod[layer], lambda b: dec_b, None, layer, wts,
                                 tm_in=seq, tt=seq, tm_out=seq, tm_ffn=seq, grid_conv=False)
        ctx_states.append(s_ctx)
        xs, _ = _trunk_layer(xs, mod[layer], lambda b: b, state_delta[:, layer], layer, wts,
                             tm_in=lat_tile, tt=256, tm_out=lat_tile, tm_ffn=lat_tile,
                             grid_conv=True)
    return xp, xs, jnp.stack(ctx_states, axis=1)
```

```python
import functools

import jax
import jax.numpy as jnp
from jax import lax
from jax.experimental import pallas as pl
from jax.experimental.pallas import tpu as pltpu

F32 = jnp.float32
MM = jnp.bfloat16
EPS = 1e-6
GRID_W = 64
N_HEADS = 4
HEAD_D = 128
D_A = N_HEADS * HEAD_D
D_B = 512
SHORT_CONV = 5
CF_CONV = 31
CHUNK = 64
N_MOD = 6
HALO = 16
FF_CHUNK = 256
LANES = 128
SUBLANES = 8
SH_COLS = 512
MAX_TILE_ROWS = 512
ROWS = 64
FFN_UNROLL_MAX_ROWS = 256
FFN_CHUNKS_PER_ITER = 2
ACT_ROWS = 64
VMEM_LIMIT = 56 * 1024 * 1024


def _sigmoid(x):
    return 1.0 / (1.0 + jnp.exp(-x))


def _silu(x):
    return x * _sigmoid(x)


def _softplus(x):
    return jnp.maximum(x, 0.0) + jnp.log(1.0 + jnp.exp(-jnp.abs(x)))


def _dot(a, b):
    return jnp.dot(a.astype(MM), b.astype(MM), preferred_element_type=F32)


def _dot_nt(a, b):
    return lax.dot_general(a.astype(MM), b.astype(MM), (((1,), (1,)), ((), ())),
                           preferred_element_type=F32)


def _dot_tn(a, b):
    return lax.dot_general(a.astype(MM), b.astype(MM), (((0,), (0,)), ((), ())),
                           preferred_element_type=F32)


def _dot_exact(a, b):
    return jnp.dot(a, b, preferred_element_type=F32, precision=lax.Precision.HIGHEST)


def _rms_scale(x):
    return lax.rsqrt(jnp.mean(x * x, axis=-1, keepdims=True) + EPS)


def _mod_kernel(c_ref, w_ref, b_ref, o_ref):
    o_ref[0] = _dot_exact(_silu(c_ref[...]), w_ref[0]) + b_ref[0]


def _modulation(cc, w_mod, b_mod):
    depth, d, n = w_mod.shape
    tn = 1536
    return pl.pallas_call(
        _mod_kernel,
        grid=(depth, n // tn),
        in_specs=[
            pl.BlockSpec((8, d), lambda l, j: (0, 0)),
            pl.BlockSpec((1, d, tn), lambda l, j: (l, 0, j)),
            pl.BlockSpec((1, 1, tn), lambda l, j: (l, 0, j)),
        ],
        out_specs=pl.BlockSpec((1, 8, tn), lambda l, j: (l, 0, j)),
        out_shape=jax.ShapeDtypeStruct((depth, 8, n), F32),
        compiler_params=pltpu.CompilerParams(vmem_limit_bytes=VMEM_LIMIT),
        name="modulation",
    )(cc, w_mod, b_mod.reshape(depth, 1, n))


def _layer_spec(a, layer):
    return pl.BlockSpec((1,) + a.shape[1:], lambda b, i: (layer,) + (0,) * (a.ndim - 1))


def _mixer_in_kernel(xp_ref, xc_ref, xn_ref, mod_ref, gpre_ref, wqkv_ref, wglu_ref, wz_ref,
                     wab_ref, cq_ref, ccf_ref, bcf_ref, lng_ref, lnb_ref, alog_ref, dt_ref,
                     q_ref, k_ref, v_ref, z_ref, gb_ref, u_ref,
                     hext_ref, pext_ref, glu_ref, sh_ref, rot_ref, *, tm, nt):
    i = pl.program_id(1)
    shift = mod_ref[0, 0:1, :]
    scale1 = 1.0 + mod_ref[0, 1:2, :]
    gpre = gpre_ref[0]
    rb = ROWS
    n_ext = tm + 2 * HALO

    def norm(x):
        return ((x * _rms_scale(x)) * gpre) * scale1 + shift

    hext_ref[0:HALO, :] = jnp.where(i > 0, norm(xp_ref[0]), 0.0).astype(MM)
    hext_ref[HALO + tm:n_ext, :] = jnp.where(i < nt - 1, norm(xn_ref[0]), 0.0).astype(MM)
    for r in range(0, tm, rb):
        hext_ref[HALO + r:HALO + r + rb, :] = norm(xc_ref[0, r:r + rb, :]).astype(MM)

    hcur = hext_ref[HALO:HALO + tm, :]
    z_ref[0] = jnp.dot(hcur, wz_ref[0], preferred_element_type=F32)
    ab = jnp.dot(hcur, wab_ref[0], preferred_element_type=F32)
    g = -jnp.exp(alog_ref[0]) * _softplus(ab + dt_ref[0])
    col = lax.broadcasted_iota(jnp.int32, ab.shape, 1)
    gb_ref[0] = jnp.where(col < 2 * N_HEADS, g, _sigmoid(ab))

    pext_ref[:, 0:3 * D_A] = jnp.dot(hext_ref[...], wqkv_ref[0], preferred_element_type=F32)
    glu_ref[...] = jnp.dot(hext_ref[...], wglu_ref[0], preferred_element_type=F32)
    for r in range(0, n_ext, rb):
        rows = min(rb, n_ext - r)
        pext_ref[r:r + rows, 3 * D_A:] = (glu_ref[r:r + rows, 0:D_B]
                                          * _sigmoid(glu_ref[r:r + rows, D_B:]))

    def dwconv(c0, w_ref, wc0, width, r):
        acc = None
        for t in range(width):
            off = HALO + r + t - width // 2
            res = off % SUBLANES
            if res == 0:
                val = pext_ref[off:off + rb, c0:c0 + SH_COLS]
            else:
                val = sh_ref[res - 1, off - res:off - res + rb, :]
            term = val.reshape(rb // SUBLANES, SUBLANES, SH_COLS) * w_ref[0, t, :, wc0:wc0 + SH_COLS]
            acc = term if acc is None else acc + term
        return acc.reshape(rb, SH_COLS)

    def shifted_copies(c0, width):
        used = sorted({(HALO + t - width // 2) % SUBLANES for t in range(width)} - {0})
        groups = n_ext // SUBLANES
        x3 = pext_ref[:, c0:c0 + SH_COLS].reshape(groups, SUBLANES, SH_COLS)
        sub = lax.broadcasted_iota(jnp.int32, (groups - 1, SUBLANES, SH_COLS), 1)
        for s in used:
            rot_ref[...] = pltpu.roll(x3, SUBLANES - s, 1)
            out = jnp.where(sub < SUBLANES - s, rot_ref[0:groups - 1], rot_ref[1:groups])
            sh_ref[s - 1, :, :] = out.reshape(n_ext - SUBLANES, SH_COLS)

    for part, out_ref in enumerate((q_ref, k_ref, v_ref)):
        shifted_copies(part * D_A, SHORT_CONV)
        for r in range(0, tm, rb):
            acc = _silu(dwconv(part * D_A, cq_ref, part * D_A, SHORT_CONV, r))
            if part < 2:
                heads = []
                for h in range(N_HEADS):
                    a = acc[:, h * HEAD_D:(h + 1) * HEAD_D]
                    a = a * lax.rsqrt(jnp.sum(a * a, axis=-1, keepdims=True) + EPS)
                    heads.append(a * (HEAD_D ** -0.5) if part == 0 else a)
                acc = jnp.concatenate(heads, axis=-1)
            out_ref[0, r:r + rb, :] = acc

    shifted_copies(3 * D_A, CF_CONV)
    for r in range(0, tm, rb):
        acc = dwconv(3 * D_A, ccf_ref, 0, CF_CONV, r) + bcf_ref[0]
        mu = jnp.mean(acc, axis=-1, keepdims=True)
        cen = acc - mu
        var = jnp.mean(cen * cen, axis=-1, keepdims=True)
        y = cen * lax.rsqrt(var + EPS) * lng_ref[0] + lnb_ref[0]
        u_ref[0, r:r + rb, :] = _silu(y).astype(u_ref.dtype)


def _mixer_in(x, mod, mrow, layer, wts, *, tm):
    bsz, t, d = x.shape
    nt = t // tm
    hb = tm // HALO
    nhb = t // HALO
    names = ("g_pre_mix", "w_qkv", "w_glu", "w_z", "w_ab", "conv_qkv", "conv_cf", "b_conv_cf",
             "ln_cf_g", "ln_cf_b", "a_log", "dt_bias")
    seq = lambda c: pl.BlockSpec((1, tm, c), lambda b, i: (b, i, 0))
    return pl.pallas_call(
        functools.partial(_mixer_in_kernel, tm=tm, nt=nt),
        grid=(bsz, nt),
        in_specs=[
            pl.BlockSpec((1, HALO, d), lambda b, i: (b, jnp.maximum(i * hb - 1, 0), 0)),
            pl.BlockSpec((1, tm, d), lambda b, i: (b, i, 0)),
            pl.BlockSpec((1, HALO, d), lambda b, i: (b, jnp.minimum((i + 1) * hb, nhb - 1), 0)),
            pl.BlockSpec((1, N_MOD, d), lambda b, i: (mrow(b), 0, 0)),
        ] + [_layer_spec(wts[n], layer) for n in names],
        out_specs=[seq(D_A), seq(D_A), seq(D_A), seq(D_A), seq(LANES), seq(D_B)],
        out_shape=[
            jax.ShapeDtypeStruct((bsz, t, D_A), F32),
            jax.ShapeDtypeStruct((bsz, t, D_A), F32),
            jax.ShapeDtypeStruct((bsz, t, D_A), F32),
            jax.ShapeDtypeStruct((bsz, t, D_A), F32),
            jax.ShapeDtypeStruct((bsz, t, LANES), F32),
            jax.ShapeDtypeStruct((bsz, t, D_B), MM),
        ],
        scratch_shapes=[
            pltpu.VMEM((tm + 2 * HALO, d), MM),
            pltpu.VMEM((tm + 2 * HALO, 3 * D_A + D_B), F32),
            pltpu.VMEM((tm + 2 * HALO, 2 * D_B), F32),
            pltpu.VMEM((SUBLANES - 1, tm + 2 * HALO - SUBLANES, SH_COLS), F32),
            pltpu.VMEM(((tm + 2 * HALO) // SUBLANES, SUBLANES, SH_COLS), F32),
        ],
        compiler_params=pltpu.CompilerParams(vmem_limit_bytes=VMEM_LIMIT),
        name="mixer_in",
    )(x, x, x, mod, *[wts[n] for n in names])


def _dn_masks(d):
    n = N_HEADS * CHUNK
    ii = lax.broadcasted_iota(jnp.int32, (n, n), 0)
    jj = lax.broadcasted_iota(jnp.int32, (n, n), 1)
    lo, hi = (jj, ii) if d == 0 else (ii, jj)
    return ii, jj, lo, hi


def _dn_prep(q_ref, k_ref, v_ref, gb_ref, r0, d, p, amat_ref, tinv_ref, qk_ref, rhs_ref,
             qg_ref, kd_ref, egt_ref):
    rows = slice(r0, r0 + CHUNK)
    log2_chunk = CHUNK.bit_length() - 1
    gbc = gb_ref[0, rows, :]
    ci = lax.broadcasted_iota(jnp.int32, (CHUNK, CHUNK), 0)
    cj = lax.broadcasted_iota(jnp.int32, (CHUNK, CHUNK), 1)
    tri = (ci >= cj) if d == 0 else (ci <= cj)
    gc = _dot_exact(jnp.where(tri, 1.0, 0.0), gbc)
    last = CHUNK - 1 if d == 0 else 0
    gtot = gc[last:last + 1, :]
    egc = jnp.exp(gc)
    edk = jnp.exp(gtot - gc)
    egt_ref[p] = jnp.exp(gtot)
    gct = jnp.concatenate([gc, jnp.zeros_like(gc)], axis=0).T

    def stack_cols(a, c0):
        return jnp.concatenate([a[:, c0 + h:c0 + h + 1] for h in range(N_HEADS)], axis=0)

    c0 = d * N_HEADS
    gc_col = stack_cols(gc, c0)
    be_col = stack_cols(gbc, 2 * N_HEADS + c0)
    egc_col = stack_cols(egc, c0)
    edk_col = stack_cols(edk, c0)
    gc_row = jnp.concatenate([gct[c0 + h:c0 + h + 1, 0:CHUNK] for h in range(N_HEADS)], axis=1)

    def stack_heads(ref):
        return jnp.concatenate(
            [ref[0, rows, h * HEAD_D:(h + 1) * HEAD_D] for h in range(N_HEADS)], axis=0)

    qs, ks, vs = stack_heads(q_ref), stack_heads(k_ref), stack_heads(v_ref)
    ii, jj, lo, hi = _dn_masks(d)
    same = (ii >> log2_chunk) == (jj >> log2_chunk)
    incl = jnp.logical_and(same, lo <= hi)
    decay = jnp.where(incl, jnp.exp(jnp.where(incl, gc_col - gc_row, 0.0)), 0.0)
    kb = ks * be_col
    amat = _dot_nt(kb, ks) * decay
    amat_ref[p] = amat
    pair = jnp.logical_and((lo >> 1) == (hi >> 1), lo < hi)
    tinv_ref[p] = jnp.where(ii == jj, 1.0, 0.0) - jnp.where(pair, amat, 0.0)
    qk_ref[p] = (_dot_nt(qs, ks) * decay).astype(MM)
    rhs_ref[p] = jnp.concatenate([vs * be_col, kb * egc_col], axis=1).astype(MM)
    qg_ref[p] = (qs * egc_col).astype(MM)
    kd_ref[p] = (ks * edk_col).astype(MM)


def _deltanet_kernel(*refs, nc, zero_init):
    if zero_init:
        qf, kf, vf, gf, qb, kb, vb, gb, of_ref, ob_ref, sfin_ref = refs[:11]
    else:
        qf, kf, vf, gf, qb, kb, vb, gb, s0_ref, of_ref, ob_ref, sfin_ref = refs[:12]
    amat_ref, tinv_ref, qk_ref, rhs_ref, qg_ref, kd_ref, egt_ref, uw_ref, s_ref = refs[-9:]
    j = pl.program_id(1)
    log2_chunk = CHUNK.bit_length() - 1
    n = N_HEADS * CHUNK

    @pl.when(j == 0)
    def _():
        if zero_init:
            s_ref[...] = jnp.zeros_like(s_ref)
        else:
            s_ref[...] = s0_ref[0]

    probs = []
    for c in range(nc):
        probs.append((0, c * CHUNK, (qf, kf, vf, gf), of_ref))
        probs.append((1, (nc - 1 - c) * CHUNK, (qb, kb, vb, gb), ob_ref))

    for p, (d, r0, (q_ref, k_ref, v_ref, g_ref), _) in enumerate(probs):
        _dn_prep(q_ref, k_ref, v_ref, g_ref, r0, d, p, amat_ref, tinv_ref, qk_ref, rhs_ref,
                 qg_ref, kd_ref, egt_ref)

    for lb in range(1, log2_chunk):
        e_masks = []
        for d in (0, 1):
            _, _, lo, hi = _dn_masks(d)
            e_masks.append(jnp.logical_and((lo >> (lb + 1)) == (hi >> (lb + 1)),
                                           (lo >> lb) < (hi >> lb)))
        b = 1 << lb
        for p, (d, _, _, _) in enumerate(probs):
            tb = tinv_ref[p].astype(MM)
            e = jnp.where(e_masks[d], amat_ref[p], 0.0)
            if b < SUBLANES:
                tinv_ref[p] = tinv_ref[p] - _dot(_dot(tb, e), tb)
            else:
                starts = [s + (b if d == 0 else 0) for s in range(0, n, 2 * b)]
                sel = jnp.concatenate([tinv_ref[p, s:s + b, :] for s in starts], axis=0)
                x = _dot(_dot(sel, e), tb)
                for idx, s in enumerate(starts):
                    tinv_ref[p, s:s + b, :] = sel[idx * b:(idx + 1) * b] - x[idx * b:(idx + 1) * b]

    for p in range(len(probs)):
        uw_ref[p] = _dot(tinv_ref[p], rhs_ref[p])

    for p, (d, r0, _, o_ref) in enumerate(probs):
        c0 = d * N_HEADS
        vnew, o1 = [], []
        for h in range(N_HEADS):
            hs = slice(h * CHUNK, (h + 1) * CHUNK)
            wq = jnp.concatenate([uw_ref[p, hs, HEAD_D:].astype(MM), qg_ref[p, hs, :]], axis=0)
            r = _dot(wq, s_ref[d, h])
            vnew.append(uw_ref[p, hs, 0:HEAD_D] - r[:CHUNK])
            o1.append(r[CHUNK:])
        vnew = jnp.concatenate(vnew, axis=0).astype(MM)
        o = jnp.concatenate(o1, axis=0) + _dot(qk_ref[p], vnew)
        for h in range(N_HEADS):
            hs = slice(h * CHUNK, (h + 1) * CHUNK)
            s_ref[d, h] = (s_ref[d, h] * egt_ref[p, :, c0 + h:c0 + h + 1]
                           + _dot_tn(kd_ref[p, hs, :], vnew[hs]))
            o_ref[0, r0:r0 + CHUNK, h * HEAD_D:(h + 1) * HEAD_D] = o[hs]

    @pl.when(j == pl.num_programs(1) - 1)
    def _():
        sfin_ref[0] = s_ref[...]


def _deltanet(q, k, v, gb, s0, *, tt):
    bsz, t, _ = q.shape
    nt = t // tt
    nc = tt // CHUNK
    n = N_HEADS * CHUNK
    zero_init = s0 is None
    fwd = lambda c: pl.BlockSpec((1, tt, c), lambda b, j: (b, j, 0))
    bwd = lambda c: pl.BlockSpec((1, tt, c), lambda b, j: (b, nt - 1 - j, 0))
    st_spec = pl.BlockSpec((1, 2, N_HEADS, HEAD_D, HEAD_D), lambda b, j: (b, 0, 0, 0, 0))
    in_specs = [fwd(D_A), fwd(D_A), fwd(D_A), fwd(LANES), bwd(D_A), bwd(D_A), bwd(D_A), bwd(LANES)]
    args = [q, k, v, gb, q, k, v, gb]
    if not zero_init:
        in_specs.append(st_spec)
        args.append(s0)
    n_prob = 2 * nc
    return pl.pallas_call(
        functools.partial(_deltanet_kernel, nc=nc, zero_init=zero_init),
        grid=(bsz, nt),
        in_specs=in_specs,
        out_specs=[fwd(D_A), bwd(D_A), st_spec],
        out_shape=[
            jax.ShapeDtypeStruct((bsz, t, D_A), F32),
            jax.ShapeDtypeStruct((bsz, t, D_A), F32),
            jax.ShapeDtypeStruct((bsz, 2, N_HEADS, HEAD_D, HEAD_D), F32),
        ],
        scratch_shapes=[
            pltpu.VMEM((n_prob, n, n), F32),
            pltpu.VMEM((n_prob, n, n), F32),
            pltpu.VMEM((n_prob, n, n), MM),
            pltpu.VMEM((n_prob, n, 2 * HEAD_D), MM),
            pltpu.VMEM((n_prob, n, HEAD_D), MM),
            pltpu.VMEM((n_prob, n, HEAD_D), MM),
            pltpu.VMEM((n_prob, 1, LANES), F32),
            pltpu.VMEM((n_prob, n, 2 * HEAD_D), F32),
            pltpu.VMEM((2, N_HEADS, HEAD_D, HEAD_D), F32),
        ],
        compiler_params=pltpu.CompilerParams(
            dimension_semantics=("parallel", "arbitrary"), vmem_limit_bytes=VMEM_LIMIT),
        name="deltanet",
    )(*args)


def _mixer_out_kernel(of_ref, ob_ref, z_ref, u_ref, x_ref, mod_ref, gon_ref, wo_ref, gpost_ref,
                      out_ref, cat_ref, mix_ref, *, tm):
    rb = ROWS
    gate = mod_ref[0, 2:3, :]
    for r in range(0, tm, rb):
        rows = slice(r, r + rb)
        o = of_ref[0, rows, :] + ob_ref[0, rows, :]
        z = z_ref[0, rows, :]
        for h in range(N_HEADS):
            hs = slice(h * HEAD_D, (h + 1) * HEAD_D)
            oh = o[:, hs]
            oh = oh * _rms_scale(oh) * gon_ref[0]
            cat_ref[rows, hs] = (oh * _silu(z[:, hs])).astype(MM)
        cat_ref[rows, D_A:] = u_ref[0, rows, :]
    mix_ref[...] = jnp.dot(cat_ref[...], wo_ref[0], preferred_element_type=F32)
    for r in range(0, tm, rb):
        rows = slice(r, r + rb)
        mix = mix_ref[rows, :]
        mix = mix * _rms_scale(mix) * gpost_ref[0]
        out_ref[0, rows, :] = x_ref[0, rows, :] + gate * mix


def _mixer_out(o_f, o_b, z, u, x, mod, mrow, layer, wts, *, tm):
    bsz, t, d = x.shape
    names = ("g_onorm", "w_out", "g_post_mix")
    seq = lambda c: pl.BlockSpec((1, tm, c), lambda b, i: (b, i, 0))
    return pl.pallas_call(
        functools.partial(_mixer_out_kernel, tm=tm),
        grid=(bsz, t // tm),
        in_specs=[
            seq(D_A), seq(D_A), seq(D_A), seq(D_B), seq(d),
            pl.BlockSpec((1, N_MOD, d), lambda b, i: (mrow(b), 0, 0)),
        ] + [_layer_spec(wts[n], layer) for n in names],
        out_specs=seq(d),
        out_shape=jax.ShapeDtypeStruct((bsz, t, d), F32),
        scratch_shapes=[pltpu.VMEM((tm, D_A + D_B), MM), pltpu.VMEM((tm, d), F32)],
        compiler_params=pltpu.CompilerParams(vmem_limit_bytes=VMEM_LIMIT),
        name="mixer_out",
    )(o_f, o_b, z, u, x, mod, *[wts[n] for n in names])


def _ffn_kernel(xp_ref, xc_ref, xn_ref, mod_ref, gpre_ref, wg_ref, wu_ref, wd_ref, cw_ref,
                cb_ref, gpost_ref, out_ref, hext_ref, gts0_ref, gts1_ref, up0_ref, up1_ref,
                act0_ref, act1_ref, y_ref, *, tm, nt, halo, width, taps, n_chunks, unroll_all):
    i = pl.program_id(1)
    pad = 8
    shift = mod_ref[0, 3:4, :]
    scale1 = 1.0 + mod_ref[0, 4:5, :]
    gate = mod_ref[0, 5:6, :]
    gpre = gpre_ref[0]
    rb = ROWS
    n_ext = tm + 2 * halo

    def norm(x):
        return ((x * _rms_scale(x)) * gpre) * scale1 + shift

    if halo:
        hext_ref[0:halo, :] = jnp.where(i > 0, norm(xp_ref[0]), 0.0).astype(MM)
        hext_ref[halo + tm:n_ext, :] = jnp.where(i < nt - 1, norm(xn_ref[0]), 0.0).astype(MM)
    for r in range(0, tm, rb):
        hext_ref[halo + r:halo + r + rb, :] = norm(xc_ref[0, r:r + rb, :]).astype(MM)

    gts, ups, acts = (gts0_ref, gts1_ref), (up0_ref, up1_ref), (act0_ref, act1_ref)
    for g_ref in gts:
        g_ref[0:pad, :] = jnp.zeros((pad, FF_CHUNK), F32)
        g_ref[pad + n_ext:pad + n_ext + pad, :] = jnp.zeros((pad, FF_CHUNK), F32)
    y_ref[...] = jnp.zeros_like(y_ref)

    def project(c, slot):
        gts[slot][pad:pad + n_ext, :] = jnp.dot(hext_ref[...], wg_ref[0, c],
                                                preferred_element_type=F32)
        ups[slot][...] = jnp.dot(hext_ref[halo:halo + tm, :], wu_ref[0, c],
                                 preferred_element_type=F32)

    def activate(c, slot, rb=ACT_ROWS):
        for r in range(0, tm, rb):
            pos = (r + lax.broadcasted_iota(jnp.int32, (rb, 1), 0)) % width
            acc = None
            for dr, dc, widx in taps:
                s = pad + halo + r + dr * GRID_W + dc
                val = gts[slot][s:s + rb, :]
                if dc != 0:
                    ok = (pos >= 1) if dc < 0 else (pos <= width - 2)
                    val = jnp.where(ok, val, 0.0)
                term = val.reshape(rb // SUBLANES, SUBLANES, FF_CHUNK) * cw_ref[0, c, widx]
                acc = term if acc is None else acc + term
            gt = acc.reshape(rb, FF_CHUNK) + cb_ref[0, c]
            acts[slot][r:r + rb, :] = (_silu(gt) * ups[slot][r:r + rb, :]).astype(MM)

    def contract(c, slot):
        y_ref[...] += jnp.dot(acts[slot][...], wd_ref[0, c], preferred_element_type=F32)

    def steady(c, slot):
        activate(c, slot)
        project(c + 1, 1 - slot)
        contract(c, slot)

    project(0, 0)
    n_steady = n_chunks - 1
    if unroll_all:
        n_looped = 0
    else:
        per_iter = FFN_CHUNKS_PER_ITER
        n_looped = n_steady - n_steady % per_iter

        def loop_body(k, carry):
            for s in range(per_iter):
                steady(per_iter * k + s, s % 2)
            return carry

        lax.fori_loop(0, n_looped // per_iter, loop_body, 0)
    for c in range(n_looped, n_steady):
        steady(c, c % 2)
    last = n_chunks - 1
    activate(last, last % 2)
    contract(last, last % 2)

    for r in range(0, tm, rb):
        y = y_ref[r:r + rb, :]
        y = y * _rms_scale(y) * gpost_ref[0]
        out_ref[0, r:r + rb, :] = xc_ref[0, r:r + rb, :] + gate * y


def _ffn(x, mod, mrow, layer, wts, *, tm, grid_conv):
    bsz, t, d = x.shape
    nt = t // tm
    n_chunks = wts["w_gate"].shape[1]
    if grid_conv:
        halo = GRID_W
        width = GRID_W
        taps = tuple((dr, dc, (dr + 1) * 3 + (dc + 1)) for dr in (-1, 0, 1) for dc in (-1, 0, 1))
        hb, nhb = tm // halo, t // halo
        xp_spec = pl.BlockSpec((1, halo, d), lambda b, i: (b, jnp.maximum(i * hb - 1, 0), 0))
        xn_spec = pl.BlockSpec((1, halo, d), lambda b, i: (b, jnp.minimum((i + 1) * hb, nhb - 1), 0))
    else:
        assert tm == t
        halo = 0
        width = t
        taps = tuple((0, dc, 3 + (dc + 1)) for dc in (-1, 0, 1))
        xp_spec = pl.BlockSpec((1, 8, d), lambda b, i: (b, 0, 0))
        xn_spec = pl.BlockSpec((1, 8, d), lambda b, i: (b, 0, 0))
    names = ("g_pre_ffn", "w_gate", "w_upp", "w_down", "conv_ffn", "b_conv_ffn", "g_post_ffn")
    return pl.pallas_call(
        functools.partial(_ffn_kernel, tm=tm, nt=nt, halo=halo, width=width, taps=taps,
                          n_chunks=n_chunks, unroll_all=tm <= FFN_UNROLL_MAX_ROWS),
        grid=(bsz, nt),
        in_specs=[
            xp_spec,
            pl.BlockSpec((1, tm, d), lambda b, i: (b, i, 0)),
            xn_spec,
            pl.BlockSpec((1, N_MOD, d), lambda b, i: (mrow(b), 0, 0)),
        ] + [_layer_spec(wts[n], layer) for n in names],
        out_specs=pl.BlockSpec((1, tm, d), lambda b, i: (b, i, 0)),
        out_shape=jax.ShapeDtypeStruct((bsz, t, d), F32),
        scratch_shapes=[
            pltpu.VMEM((tm + 2 * halo, d), MM),
            pltpu.VMEM((tm + 2 * halo + 16, FF_CHUNK), F32),
            pltpu.VMEM((tm + 2 * halo + 16, FF_CHUNK), F32),
            pltpu.VMEM((tm, FF_CHUNK), F32),
            pltpu.VMEM((tm, FF_CHUNK), F32),
            pltpu.VMEM((tm, FF_CHUNK), MM),
            pltpu.VMEM((tm, FF_CHUNK), MM),
            pltpu.VMEM((tm, d), F32),
        ],
        compiler_params=pltpu.CompilerParams(vmem_limit_bytes=VMEM_LIMIT),
        name="ffn",
    )(x, x, x, mod, *[wts[n] for n in names])


def _prepare_weights(w_in, conv_qkv, a_log, dt_bias, g_onorm, conv_cf, b_conv_cf, ln_cf_g,
                     ln_cf_b, w_out, w_up, conv_ffn, b_conv_ffn, w_down, g_pre_mix, g_post_mix,
                     g_pre_ffn, g_post_ffn):
    depth, d, _ = w_in.shape
    d_ff = w_down.shape[1]
    n_chunks = d_ff // FF_CHUNK
    n_ab = 4 * N_HEADS
    ab = w_in[:, :, 4 * D_A:4 * D_A + n_ab]
    vec = lambda a: a.reshape(depth, 1, -1)
    rows8 = lambda a: jnp.broadcast_to(a[..., None, :], a.shape[:-1] + (SUBLANES, a.shape[-1]))
    pad_lane = lambda a: jnp.pad(a.reshape(depth, 1, -1),
                                 ((0, 0), (0, 0), (0, LANES - 2 * N_HEADS)))
    chunked = lambda w: jnp.transpose(w.reshape(depth, d, n_chunks, FF_CHUNK), (0, 2, 1, 3))
    return {
        "w_qkv": w_in[:, :, :3 * D_A].astype(MM),
        "w_z": w_in[:, :, 3 * D_A:4 * D_A].astype(MM),
        "w_ab": jnp.pad(ab, ((0, 0), (0, 0), (0, LANES - n_ab))).astype(MM),
        "w_glu": w_in[:, :, 4 * D_A + n_ab:].astype(MM),
        "conv_qkv": rows8(conv_qkv), "conv_cf": rows8(conv_cf), "b_conv_cf": vec(b_conv_cf),
        "ln_cf_g": vec(ln_cf_g), "ln_cf_b": vec(ln_cf_b),
        "a_log": pad_lane(a_log), "dt_bias": pad_lane(dt_bias),
        "g_onorm": vec(g_onorm), "w_out": w_out.astype(MM),
        "w_gate": chunked(w_up[:, :, :d_ff]).astype(MM),
        "w_upp": chunked(w_up[:, :, d_ff:]).astype(MM),
        "w_down": w_down.reshape(depth, n_chunks, FF_CHUNK, d).astype(MM),
        "conv_ffn": rows8(jnp.transpose(
            conv_ffn.reshape(depth, 9, n_chunks, FF_CHUNK), (0, 2, 1, 3))),
        "b_conv_ffn": b_conv_ffn.reshape(depth, n_chunks, 1, FF_CHUNK),
        "g_pre_mix": vec(g_pre_mix), "g_post_mix": vec(g_post_mix),
        "g_pre_ffn": vec(g_pre_ffn), "g_post_ffn": vec(g_post_ffn),
    }


def _tiles(seq_len):
    tile = min(MAX_TILE_ROWS, seq_len)
    assert seq_len % tile == 0 and tile % (2 * CHUNK) == 0 and tile % GRID_W == 0
    return tile


def _trunk_layer(x, mod, mrow, s0, layer, wts, *, grid_conv):
    tile = _tiles(x.shape[1])
    q, k, v, z, gb, u = _mixer_in(x, mod, mrow, layer, wts, tm=tile)
    o_f, o_b, s_fin = _deltanet(q, k, v, gb, s0, tt=tile)
    x = _mixer_out(o_f, o_b, z, u, x, mod, mrow, layer, wts, tm=tile)
    x = _ffn(x, mod, mrow, layer, wts, tm=tile, grid_conv=grid_conv)
    return x, s_fin


def kernel(x_prompt, x_sample, state_delta, c, c_ctx, w_mod, b_mod, g_pre_mix, g_post_mix,
           g_pre_ffn, g_post_ffn, w_in, conv_qkv, a_log, dt_bias, g_onorm, conv_cf, b_conv_cf,
           ln_cf_g, ln_cf_b, w_out, w_up, conv_ffn, b_conv_ffn, w_down):
    depth, d = g_pre_mix.shape
    dec_b = c.shape[0]
    assert dec_b < 8
    cc = jnp.concatenate([c, c_ctx[None, :], jnp.zeros((8 - dec_b - 1, d), c.dtype)], axis=0)
    mod = _modulation(cc, w_mod, b_mod).reshape(depth, 8, N_MOD, d)
    wts = _prepare_weights(w_in, conv_qkv, a_log, dt_bias, g_onorm, conv_cf, b_conv_cf, ln_cf_g,
                           ln_cf_b, w_out, w_up, conv_ffn, b_conv_ffn, w_down, g_pre_mix,
                           g_post_mix, g_pre_ffn, g_post_ffn)
    xp, xs = x_prompt, x_sample
    ctx_states = []
    for layer in range(depth):
        xp, s_ctx = _trunk_layer(xp, mod[layer], lambda b: dec_b, None, layer, wts,
                                 grid_conv=False)
        ctx_states.append(s_ctx)
        xs, _ = _trunk_layer(xs, mod[layer], lambda b: b, state_delta[:, layer], layer, wts,
                             grid_conv=True)
    return xp, xs, jnp.stack(ctx_states, axis=1)
```

```python
import functools

import jax
import jax.numpy as jnp
from jax import lax
from jax.experimental import pallas as pl
from jax.experimental.pallas import tpu as pltpu

F32 = jnp.float32
MM = jnp.bfloat16
EPS = 1e-6
GRID_W = 64
N_HEADS = 4
HEAD_D = 128
D_A = N_HEADS * HEAD_D
D_B = 512
SHORT_CONV = 5
CF_CONV = 31
CHUNK = 64
N_MOD = 6
HALO = 16
FF_CHUNK = 256
LANES = 128
SUBLANES = 8
SH_COLS = 512
MAX_TILE_ROWS = 512
ROWS = 64
FFN_UNROLL_MAX_ROWS = 256
FFN_CHUNKS_PER_ITER = 2
ACT_ROWS = 64
VMEM_LIMIT = 56 * 1024 * 1024


def _sigmoid(x):
    return 1.0 / (1.0 + jnp.exp(-x))


def _silu(x):
    return x * _sigmoid(x)


def _softplus(x):
    return jnp.maximum(x, 0.0) + jnp.log(1.0 + jnp.exp(-jnp.abs(x)))


def _dot(a, b):
    return jnp.dot(a.astype(MM), b.astype(MM), preferred_element_type=F32)


def _dot_nt(a, b):
    return lax.dot_general(a.astype(MM), b.astype(MM), (((1,), (1,)), ((), ())),
                           preferred_element_type=F32)


def _dot_tn(a, b):
    return lax.dot_general(a.astype(MM), b.astype(MM), (((0,), (0,)), ((), ())),
                           preferred_element_type=F32)


def _dot_exact(a, b):
    return jnp.dot(a, b, preferred_element_type=F32, precision=lax.Precision.HIGHEST)


def _rms_scale(x):
    return lax.rsqrt(jnp.mean(x * x, axis=-1, keepdims=True) + EPS)


def _mod_kernel(c_ref, w_ref, b_ref, o_ref):
    o_ref[0] = _dot_exact(_silu(c_ref[...]), w_ref[0]) + b_ref[0]


def _modulation(cc, w_mod, b_mod):
    depth, d, n = w_mod.shape
    tn = 1536
    return pl.pallas_call(
        _mod_kernel,
        grid=(depth, n // tn),
        in_specs=[
            pl.BlockSpec((8, d), lambda l, j: (0, 0)),
            pl.BlockSpec((1, d, tn), lambda l, j: (l, 0, j)),
            pl.BlockSpec((1, 1, tn), lambda l, j: (l, 0, j)),
        ],
        out_specs=pl.BlockSpec((1, 8, tn), lambda l, j: (l, 0, j)),
        out_shape=jax.ShapeDtypeStruct((depth, 8, n), F32),
        compiler_params=pltpu.CompilerParams(vmem_limit_bytes=VMEM_LIMIT),
        name="modulation",
    )(cc, w_mod, b_mod.reshape(depth, 1, n))


def _layer_spec(a, layer):
    return pl.BlockSpec((1,) + a.shape[1:], lambda b, i: (layer,) + (0,) * (a.ndim - 1))


def _mixer_in_kernel(xp_ref, xc_ref, xn_ref, mod_ref, gpre_ref, wqkv_ref, wglu_ref, wz_ref,
                     wab_ref, cq_ref, ccf_ref, bcf_ref, lng_ref, lnb_ref, alog_ref, dt_ref,
                     q_ref, k_ref, v_ref, z_ref, gb_ref, u_ref,
                     hext_ref, pext_ref, glu_ref, sh_ref, rot_ref, *, tm, nt):
    i = pl.program_id(1)
    shift = mod_ref[0, 0:1, :]
    scale1 = 1.0 + mod_ref[0, 1:2, :]
    gpre = gpre_ref[0]
    rb = ROWS
    n_ext = tm + 2 * HALO

    def norm(x):
        return ((x * _rms_scale(x)) * gpre) * scale1 + shift

    hext_ref[0:HALO, :] = jnp.where(i > 0, norm(xp_ref[0]), 0.0).astype(MM)
    hext_ref[HALO + tm:n_ext, :] = jnp.where(i < nt - 1, norm(xn_ref[0]), 0.0).astype(MM)
    for r in range(0, tm, rb):
        hext_ref[HALO + r:HALO + r + rb, :] = norm(xc_ref[0, r:r + rb, :]).astype(MM)

    hcur = hext_ref[HALO:HALO + tm, :]
    z_ref[0] = jnp.dot(hcur, wz_ref[0], preferred_element_type=F32)
    ab = jnp.dot(hcur, wab_ref[0], preferred_element_type=F32)
    g = -jnp.exp(alog_ref[0]) * _softplus(ab + dt_ref[0])
    col = lax.broadcasted_iota(jnp.int32, ab.shape, 1)
    gb_ref[0] = jnp.where(col < 2 * N_HEADS, g, _sigmoid(ab))

    pext_ref[:, 0:3 * D_A] = jnp.dot(hext_ref[...], wqkv_ref[0], preferred_element_type=F32)
    glu_ref[...] = jnp.dot(hext_ref[...], wglu_ref[0], preferred_element_type=F32)
    for r in range(0, n_ext, rb):
        rows = min(rb, n_ext - r)
        pext_ref[r:r + rows, 3 * D_A:] = (glu_ref[r:r + rows, 0:D_B]
                                          * _sigmoid(glu_ref[r:r + rows, D_B:]))

    def dwconv(c0, w_ref, wc0, width, r):
        acc = None
        for t in range(width):
            off = HALO + r + t - width // 2
            res = off % SUBLANES
            if res == 0:
                val = pext_ref[off:off + rb, c0:c0 + SH_COLS]
            else:
                val = sh_ref[res - 1, off - res:off - res + rb, :]
            term = val.reshape(rb // SUBLANES, SUBLANES, SH_COLS) * w_ref[0, t, :, wc0:wc0 + SH_COLS]
            acc = term if acc is None else acc + term
        return acc.reshape(rb, SH_COLS)

    def shifted_copies(c0, width):
        used = sorted({(HALO + t - width // 2) % SUBLANES for t in range(width)} - {0})
        groups = n_ext // SUBLANES
        x3 = pext_ref[:, c0:c0 + SH_COLS].reshape(groups, SUBLANES, SH_COLS)
        sub = lax.broadcasted_iota(jnp.int32, (groups - 1, SUBLANES, SH_COLS), 1)
        for s in used:
            rot_ref[...] = pltpu.roll(x3, SUBLANES - s, 1)
            out = jnp.where(sub < SUBLANES - s, rot_ref[0:groups - 1], rot_ref[1:groups])
            sh_ref[s - 1, :, :] = out.reshape(n_ext - SUBLANES, SH_COLS)

    for part, out_ref in enumerate((q_ref, k_ref, v_ref)):
        shifted_copies(part * D_A, SHORT_CONV)
        for r in range(0, tm, rb):
            acc = _silu(dwconv(part * D_A, cq_ref, part * D_A, SHORT_CONV, r))
            if part < 2:
                heads = []
                for h in range(N_HEADS):
                    a = acc[:, h * HEAD_D:(h + 1) * HEAD_D]
                    a = a * lax.rsqrt(jnp.sum(a * a, axis=-1, keepdims=True) + EPS)
                    heads.append(a * (HEAD_D ** -0.5) if part == 0 else a)
                acc = jnp.concatenate(heads, axis=-1)
            out_ref[0, r:r + rb, :] = acc

    shifted_copies(3 * D_A, CF_CONV)
    for r in range(0, tm, rb):
        acc = dwconv(3 * D_A, ccf_ref, 0, CF_CONV, r) + bcf_ref[0]
        mu = jnp.mean(acc, axis=-1, keepdims=True)
        cen = acc - mu
        var = jnp.mean(cen * cen, axis=-1, keepdims=True)
        y = cen * lax.rsqrt(var + EPS) * lng_ref[0] + lnb_ref[0]
        u_ref[0, r:r + rb, :] = _silu(y).astype(u_ref.dtype)


def _mixer_in(x, mod, mrow, layer, wts, *, tm):
    bsz, t, d = x.shape
    nt = t // tm
    hb = tm // HALO
    nhb = t // HALO
    names = ("g_pre_mix", "w_qkv", "w_glu", "w_z", "w_ab", "conv_qkv", "conv_cf", "b_conv_cf",
             "ln_cf_g", "ln_cf_b", "a_log", "dt_bias")
    seq = lambda c: pl.BlockSpec((1, tm, c), lambda b, i: (b, i, 0))
    return pl.pallas_call(
        functools.partial(_mixer_in_kernel, tm=tm, nt=nt),
        grid=(bsz, nt),
        in_specs=[
            pl.BlockSpec((1, HALO, d), lambda b, i: (b, jnp.maximum(i * hb - 1, 0), 0)),
            pl.BlockSpec((1, tm, d), lambda b, i: (b, i, 0)),
            pl.BlockSpec((1, HALO, d), lambda b, i: (b, jnp.minimum((i + 1) * hb, nhb - 1), 0)),
            pl.BlockSpec((1, N_MOD, d), lambda b, i: (mrow(b), 0, 0)),
        ] + [_layer_spec(wts[n], layer) for n in names],
        out_specs=[seq(D_A), seq(D_A), seq(D_A), seq(D_A), seq(LANES), seq(D_B)],
        out_shape=[
            jax.ShapeDtypeStruct((bsz, t, D_A), F32),
            jax.ShapeDtypeStruct((bsz, t, D_A), F32),
            jax.ShapeDtypeStruct((bsz, t, D_A), F32),
            jax.ShapeDtypeStruct((bsz, t, D_A), F32),
            jax.ShapeDtypeStruct((bsz, t, LANES), F32),
            jax.ShapeDtypeStruct((bsz, t, D_B), MM),
        ],
        scratch_shapes=[
            pltpu.VMEM((tm + 2 * HALO, d), MM),
            pltpu.VMEM((tm + 2 * HALO, 3 * D_A + D_B), F32),
            pltpu.VMEM((tm + 2 * HALO, 2 * D_B), F32),
            pltpu.VMEM((SUBLANES - 1, tm + 2 * HALO - SUBLANES, SH_COLS), F32),
            pltpu.VMEM(((tm + 2 * HALO) // SUBLANES, SUBLANES, SH_COLS), F32),
        ],
        compiler_params=pltpu.CompilerParams(vmem_limit_bytes=VMEM_LIMIT),
        name="mixer_in",
    )(x, x, x, mod, *[wts[n] for n in names])


def _dn_masks(d):
    n = N_HEADS * CHUNK
    ii = lax.broadcasted_iota(jnp.int32, (n, n), 0)
    jj = lax.broadcasted_iota(jnp.int32, (n, n), 1)
    lo, hi = (jj, ii) if d == 0 else (ii, jj)
    return ii, jj, lo, hi


def _dn_prep(q_ref, k_ref, v_ref, gb_ref, r0, d, p, amat_ref, tinv_ref, qk_ref, rhs_ref,
             qg_ref, kd_ref, egt_ref):
    rows = slice(r0, r0 + CHUNK)
    log2_chunk = CHUNK.bit_length() - 1
    gbc = gb_ref[0, rows, :]
    ci = lax.broadcasted_iota(jnp.int32, (CHUNK, CHUNK), 0)
    cj = lax.broadcasted_iota(jnp.int32, (CHUNK, CHUNK), 1)
    tri = (ci >= cj) if d == 0 else (ci <= cj)
    gc = _dot_exact(jnp.where(tri, 1.0, 0.0), gbc)
    last = CHUNK - 1 if d == 0 else 0
    gtot = gc[last:last + 1, :]
    egc = jnp.exp(gc)
    edk = jnp.exp(gtot - gc)
    egt_ref[p] = jnp.exp(gtot)
    gct = jnp.concatenate([gc, jnp.zeros_like(gc)], axis=0).T

    def stack_cols(a, c0):
        return jnp.concatenate([a[:, c0 + h:c0 + h + 1] for h in range(N_HEADS)], axis=0)

    c0 = d * N_HEADS
    gc_col = stack_cols(gc, c0)
    be_col = stack_cols(gbc, 2 * N_HEADS + c0)
    egc_col = stack_cols(egc, c0)
    edk_col = stack_cols(edk, c0)
    gc_row = jnp.concatenate([gct[c0 + h:c0 + h + 1, 0:CHUNK] for h in range(N_HEADS)], axis=1)

    def stack_heads(ref):
        return jnp.concatenate(
            [ref[0, rows, h * HEAD_D:(h + 1) * HEAD_D] for h in range(N_HEADS)], axis=0)

    qs, ks, vs = stack_heads(q_ref), stack_heads(k_ref), stack_heads(v_ref)
    ii, jj, lo, hi = _dn_masks(d)
    same = (ii >> log2_chunk) == (jj >> log2_chunk)
    incl = jnp.logical_and(same, lo <= hi)
    decay = jnp.where(incl, jnp.exp(jnp.where(incl, gc_col - gc_row, 0.0)), 0.0)
    kb = ks * be_col
    amat = _dot_nt(kb, ks) * decay
    amat_ref[p] = amat
    pair = jnp.logical_and((lo >> 1) == (hi >> 1), lo < hi)
    tinv_ref[p] = jnp.where(ii == jj, 1.0, 0.0) - jnp.where(pair, amat, 0.0)
    qk_ref[p] = (_dot_nt(qs, ks) * decay).astype(MM)
    rhs_ref[p] = jnp.concatenate([vs * be_col, kb * egc_col], axis=1).astype(MM)
    qg_ref[p] = (qs * egc_col).astype(MM)
    kd_ref[p] = (ks * edk_col).astype(MM)


def _deltanet_kernel(*refs, nc, zero_init):
    if zero_init:
        qf, kf, vf, gf, qb, kb, vb, gb, of_ref, ob_ref, sfin_ref = refs[:11]
    else:
        qf, kf, vf, gf, qb, kb, vb, gb, s0_ref, of_ref, ob_ref, sfin_ref = refs[:12]
    amat_ref, tinv_ref, qk_ref, rhs_ref, qg_ref, kd_ref, egt_ref, uw_ref, s_ref = refs[-9:]
    j = pl.program_id(1)
    log2_chunk = CHUNK.bit_length() - 1
    n = N_HEADS * CHUNK

    @pl.when(j == 0)
    def _():
        if zero_init:
            s_ref[...] = jnp.zeros_like(s_ref)
        else:
            s_ref[...] = s0_ref[0]

    probs = []
    for c in range(nc):
        probs.append((0, c * CHUNK, (qf, kf, vf, gf), of_ref))
        probs.append((1, (nc - 1 - c) * CHUNK, (qb, kb, vb, gb), ob_ref))

    for p, (d, r0, (q_ref, k_ref, v_ref, g_ref), _) in enumerate(probs):
        _dn_prep(q_ref, k_ref, v_ref, g_ref, r0, d, p, amat_ref, tinv_ref, qk_ref, rhs_ref,
                 qg_ref, kd_ref, egt_ref)

    for lb in range(1, log2_chunk):
        e_masks = []
        for d in (0, 1):
            _, _, lo, hi = _dn_masks(d)
            e_masks.append(jnp.logical_and((lo >> (lb + 1)) == (hi >> (lb + 1)),
                                           (lo >> lb) < (hi >> lb)))
        b = 1 << lb
        for p, (d, _, _, _) in enumerate(probs):
            tb = tinv_ref[p].astype(MM)
            e = jnp.where(e_masks[d], amat_ref[p], 0.0)
            if b < SUBLANES:
                tinv_ref[p] = tinv_ref[p] - _dot(_dot(tb, e), tb)
            else:
                starts = [s + (b if d == 0 else 0) for s in range(0, n, 2 * b)]
                sel = jnp.concatenate([tinv_ref[p, s:s + b, :] for s in starts], axis=0)
                x = _dot(_dot(sel, e), tb)
                for idx, s in enumerate(starts):
                    tinv_ref[p, s:s + b, :] = sel[idx * b:(idx + 1) * b] - x[idx * b:(idx + 1) * b]

    for p in range(len(probs)):
        uw_ref[p] = _dot(tinv_ref[p], rhs_ref[p])

    for p, (d, r0, _, o_ref) in enumerate(probs):
        c0 = d * N_HEADS
        vnew, o1 = [], []
        for h in range(N_HEADS):
            hs = slice(h * CHUNK, (h + 1) * CHUNK)
            wq = jnp.concatenate([uw_ref[p, hs, HEAD_D:].astype(MM), qg_ref[p, hs, :]], axis=0)
            r = _dot(wq, s_ref[d, h])
            vnew.append(uw_ref[p, hs, 0:HEAD_D] - r[:CHUNK])
            o1.append(r[CHUNK:])
        vnew = jnp.concatenate(vnew, axis=0).astype(MM)
        o = jnp.concatenate(o1, axis=0) + _dot(qk_ref[p], vnew)
        for h in range(N_HEADS):
            hs = slice(h * CHUNK, (h + 1) * CHUNK)
            s_ref[d, h] = (s_ref[d, h] * egt_ref[p, :, c0 + h:c0 + h + 1]
                           + _dot_tn(kd_ref[p, hs, :], vnew[hs]))
            o_ref[0, r0:r0 + CHUNK, h * HEAD_D:(h + 1) * HEAD_D] = o[hs]

    @pl.when(j == pl.num_programs(1) - 1)
    def _():
        sfin_ref[0] = s_ref[...]


def _deltanet(q, k, v, gb, s0, *, tt):
    bsz, t, _ = q.shape
    nt = t // tt
    nc = tt // CHUNK
    n = N_HEADS * CHUNK
    zero_init = s0 is None
    fwd = lambda c: pl.BlockSpec((1, tt, c), lambda b, j: (b, j, 0))
    bwd = lambda c: pl.BlockSpec((1, tt, c), lambda b, j: (b, nt - 1 - j, 0))
    st_spec = pl.BlockSpec((1, 2, N_HEADS, HEAD_D, HEAD_D), lambda b, j: (b, 0, 0, 0, 0))
    in_specs = [fwd(D_A), fwd(D_A), fwd(D_A), fwd(LANES), bwd(D_A), bwd(D_A), bwd(D_A), bwd(LANES)]
    args = [q, k, v, gb, q, k, v, gb]
    if not zero_init:
        in_specs.append(st_spec)
        args.append(s0)
    n_prob = 2 * nc
    return pl.pallas_call(
        functools.partial(_deltanet_kernel, nc=nc, zero_init=zero_init),
        grid=(bsz, nt),
        in_specs=in_specs,
        out_specs=[fwd(D_A), bwd(D_A), st_spec],
        out_shape=[
            jax.ShapeDtypeStruct((bsz, t, D_A), F32),
            jax.ShapeDtypeStruct((bsz, t, D_A), F32),
            jax.ShapeDtypeStruct((bsz, 2, N_HEADS, HEAD_D, HEAD_D), F32),
        ],
        scratch_shapes=[
            pltpu.VMEM((n_prob, n, n), F32),
            pltpu.VMEM((n_prob, n, n), F32),
            pltpu.VMEM((n_prob, n, n), MM),
            pltpu.VMEM((n_prob, n, 2 * HEAD_D), MM),
            pltpu.VMEM((n_prob, n, HEAD_D), MM),
            pltpu.VMEM((n_prob, n, HEAD_D), MM),
            pltpu.VMEM((n_prob, 1, LANES), F32),
            pltpu.VMEM((n_prob, n, 2 * HEAD_D), F32),
            pltpu.VMEM((2, N_HEADS, HEAD_D, HEAD_D), F32),
        ],
        compiler_params=pltpu.CompilerParams(
            dimension_semantics=("parallel", "arbitrary"), vmem_limit_bytes=VMEM_LIMIT),
        name="deltanet",
    )(*args)


def _mixer_out_kernel(of_ref, ob_ref, z_ref, u_ref, x_ref, mod_ref, gon_ref, wo_ref, gpost_ref,
                      out_ref, cat_ref, mix_ref, *, tm):
    rb = ROWS
    gate = mod_ref[0, 2:3, :]
    for r in range(0, tm, rb):
        rows = slice(r, r + rb)
        o = of_ref[0, rows, :] + ob_ref[0, rows, :]
        z = z_ref[0, rows, :]
        for h in range(N_HEADS):
            hs = slice(h * HEAD_D, (h + 1) * HEAD_D)
            oh = o[:, hs]
            oh = oh * _rms_scale(oh) * gon_ref[0]
            cat_ref[rows, hs] = (oh * _silu(z[:, hs])).astype(MM)
        cat_ref[rows, D_A:] = u_ref[0, rows, :]
    mix_ref[...] = jnp.dot(cat_ref[...], wo_ref[0], preferred_element_type=F32)
    for r in range(0, tm, rb):
        rows = slice(r, r + rb)
        mix = mix_ref[rows, :]
        mix = mix * _rms_scale(mix) * gpost_ref[0]
        out_ref[0, rows, :] = x_ref[0, rows, :] + gate * mix


def _mixer_out(o_f, o_b, z, u, x, mod, mrow, layer, wts, *, tm):
    bsz, t, d = x.shape
    names = ("g_onorm", "w_out", "g_post_mix")
    seq = lambda c: pl.BlockSpec((1, tm, c), lambda b, i: (b, i, 0))
    return pl.pallas_call(
        functools.partial(_mixer_out_kernel, tm=tm),
        grid=(bsz, t // tm),
        in_specs=[
            seq(D_A), seq(D_A), seq(D_A), seq(D_B), seq(d),
            pl.BlockSpec((1, N_MOD, d), lambda b, i: (mrow(b), 0, 0)),
        ] + [_layer_spec(wts[n], layer) for n in names],
        out_specs=seq(d),
        out_shape=jax.ShapeDtypeStruct((bsz, t, d), F32),
        scratch_shapes=[pltpu.VMEM((tm, D_A + D_B), MM), pltpu.VMEM((tm, d), F32)],
        compiler_params=pltpu.CompilerParams(vmem_limit_bytes=VMEM_LIMIT),
        name="mixer_out",
    )(o_f, o_b, z, u, x, mod, *[wts[n] for n in names])


def _ffn_kernel(xp_ref, xc_ref, xn_ref, mod_ref, gpre_ref, wgu_ref, wd_ref, cw_ref,
                cb_ref, gpost_ref, out_ref, hext_ref, gts0_ref, gts1_ref, up0_ref, up1_ref,
                act0_ref, act1_ref, y_ref, *, tm, nt, halo, width, taps, n_chunks, unroll_all):
    i = pl.program_id(1)
    pad = 8
    shift = mod_ref[0, 3:4, :]
    scale1 = 1.0 + mod_ref[0, 4:5, :]
    gate = mod_ref[0, 5:6, :]
    gpre = gpre_ref[0]
    rb = ROWS
    n_ext = tm + 2 * halo

    def norm(x):
        return ((x * _rms_scale(x)) * gpre) * scale1 + shift

    if halo:
        hext_ref[0:halo, :] = jnp.where(i > 0, norm(xp_ref[0]), 0.0).astype(MM)
        hext_ref[halo + tm:n_ext, :] = jnp.where(i < nt - 1, norm(xn_ref[0]), 0.0).astype(MM)
    for r in range(0, tm, rb):
        hext_ref[halo + r:halo + r + rb, :] = norm(xc_ref[0, r:r + rb, :]).astype(MM)

    gts, ups, acts = (gts0_ref, gts1_ref), (up0_ref, up1_ref), (act0_ref, act1_ref)
    for g_ref in gts:
        g_ref[0:pad, :] = jnp.zeros((pad, FF_CHUNK), F32)
        g_ref[pad + n_ext:pad + n_ext + pad, :] = jnp.zeros((pad, FF_CHUNK), F32)
    y_ref[...] = jnp.zeros_like(y_ref)

    def project(c, slot):
        gts[slot][pad:pad + n_ext, :] = jnp.dot(hext_ref[...], wgu_ref[0, c],
                                                preferred_element_type=F32)
        ups[slot][...] = jnp.dot(hext_ref[halo:halo + tm, :], wgu_ref[0, n_chunks + c],
                                 preferred_element_type=F32)

    def activate(c, slot, rb=ACT_ROWS):
        for r in range(0, tm, rb):
            pos = (r + lax.broadcasted_iota(jnp.int32, (rb, 1), 0)) % width
            acc = None
            for dr, dc, widx in taps:
                s = pad + halo + r + dr * GRID_W + dc
                val = gts[slot][s:s + rb, :]
                if dc != 0:
                    ok = (pos >= 1) if dc < 0 else (pos <= width - 2)
                    val = jnp.where(ok, val, 0.0)
                term = val.reshape(rb // SUBLANES, SUBLANES, FF_CHUNK) * cw_ref[0, c, widx]
                acc = term if acc is None else acc + term
            gt = acc.reshape(rb, FF_CHUNK) + cb_ref[0, c]
            acts[slot][r:r + rb, :] = (_silu(gt) * ups[slot][r:r + rb, :]).astype(MM)

    def contract(c, slot):
        y_ref[...] += jnp.dot(acts[slot][...], wd_ref[0, c], preferred_element_type=F32)

    def steady(c, slot):
        activate(c, slot)
        project(c + 1, 1 - slot)
        contract(c, slot)

    project(0, 0)
    n_steady = n_chunks - 1
    if unroll_all:
        n_looped = 0
    else:
        per_iter = FFN_CHUNKS_PER_ITER
        n_looped = n_steady - n_steady % per_iter

        def loop_body(k, carry):
            for s in range(per_iter):
                steady(per_iter * k + s, s % 2)
            return carry

        lax.fori_loop(0, n_looped // per_iter, loop_body, 0)
    for c in range(n_looped, n_steady):
        steady(c, c % 2)
    last = n_chunks - 1
    activate(last, last % 2)
    contract(last, last % 2)

    for r in range(0, tm, rb):
        y = y_ref[r:r + rb, :]
        y = y * _rms_scale(y) * gpost_ref[0]
        out_ref[0, r:r + rb, :] = xc_ref[0, r:r + rb, :] + gate * y


def _ffn(x, mod, mrow, layer, wts, *, tm, grid_conv):
    bsz, t, d = x.shape
    nt = t // tm
    n_chunks = wts["w_down"].shape[1]
    if grid_conv:
        halo = GRID_W
        width = GRID_W
        taps = tuple((dr, dc, (dr + 1) * 3 + (dc + 1)) for dr in (-1, 0, 1) for dc in (-1, 0, 1))
        hb, nhb = tm // halo, t // halo
        xp_spec = pl.BlockSpec((1, halo, d), lambda b, i: (b, jnp.maximum(i * hb - 1, 0), 0))
        xn_spec = pl.BlockSpec((1, halo, d), lambda b, i: (b, jnp.minimum((i + 1) * hb, nhb - 1), 0))
    else:
        assert tm == t
        halo = 0
        width = t
        taps = tuple((0, dc, 3 + (dc + 1)) for dc in (-1, 0, 1))
        xp_spec = pl.BlockSpec((1, 8, d), lambda b, i: (b, 0, 0))
        xn_spec = pl.BlockSpec((1, 8, d), lambda b, i: (b, 0, 0))
    names = ("g_pre_ffn", "w_gu", "w_down", "conv_ffn", "b_conv_ffn", "g_post_ffn")
    return pl.pallas_call(
        functools.partial(_ffn_kernel, tm=tm, nt=nt, halo=halo, width=width, taps=taps,
                          n_chunks=n_chunks, unroll_all=tm <= FFN_UNROLL_MAX_ROWS),
        grid=(bsz, nt),
        in_specs=[
            xp_spec,
            pl.BlockSpec((1, tm, d), lambda b, i: (b, i, 0)),
            xn_spec,
            pl.BlockSpec((1, N_MOD, d), lambda b, i: (mrow(b), 0, 0)),
        ] + [_layer_spec(wts[n], layer) for n in names],
        out_specs=pl.BlockSpec((1, tm, d), lambda b, i: (b, i, 0)),
        out_shape=jax.ShapeDtypeStruct((bsz, t, d), F32),
        scratch_shapes=[
            pltpu.VMEM((tm + 2 * halo, d), MM),
            pltpu.VMEM((tm + 2 * halo + 16, FF_CHUNK), F32),
            pltpu.VMEM((tm + 2 * halo + 16, FF_CHUNK), F32),
            pltpu.VMEM((tm, FF_CHUNK), F32),
            pltpu.VMEM((tm, FF_CHUNK), F32),
            pltpu.VMEM((tm, FF_CHUNK), MM),
            pltpu.VMEM((tm, FF_CHUNK), MM),
            pltpu.VMEM((tm, d), F32),
        ],
        compiler_params=pltpu.CompilerParams(vmem_limit_bytes=VMEM_LIMIT),
        name="ffn",
    )(x, x, x, mod, *[wts[n] for n in names])


def _prepare_weights(w_in, conv_qkv, a_log, dt_bias, g_onorm, conv_cf, b_conv_cf, ln_cf_g,
                     ln_cf_b, w_out, w_up, conv_ffn, b_conv_ffn, w_down, g_pre_mix, g_post_mix,
                     g_pre_ffn, g_post_ffn):
    depth, d, _ = w_in.shape
    d_ff = w_down.shape[1]
    n_chunks = d_ff // FF_CHUNK
    n_ab = 4 * N_HEADS
    w_in = w_in.astype(MM)
    w_up = w_up.astype(MM)
    ab = w_in[:, :, 4 * D_A:4 * D_A + n_ab]
    vec = lambda a: a.reshape(depth, 1, -1)
    rows8 = lambda a: jnp.broadcast_to(a[..., None, :], a.shape[:-1] + (SUBLANES, a.shape[-1]))
    pad_lane = lambda a: jnp.pad(a.reshape(depth, 1, -1),
                                 ((0, 0), (0, 0), (0, LANES - 2 * N_HEADS)))
    return {
        "w_qkv": w_in[:, :, :3 * D_A],
        "w_z": w_in[:, :, 3 * D_A:4 * D_A],
        "w_ab": jnp.pad(ab, ((0, 0), (0, 0), (0, LANES - n_ab))),
        "w_glu": w_in[:, :, 4 * D_A + n_ab:],
        "conv_qkv": rows8(conv_qkv), "conv_cf": rows8(conv_cf), "b_conv_cf": vec(b_conv_cf),
        "ln_cf_g": vec(ln_cf_g), "ln_cf_b": vec(ln_cf_b),
        "a_log": pad_lane(a_log), "dt_bias": pad_lane(dt_bias),
        "g_onorm": vec(g_onorm), "w_out": w_out.astype(MM),
        "w_gu": jnp.transpose(w_up.reshape(depth, d, 2 * n_chunks, FF_CHUNK), (0, 2, 1, 3)),
        "w_down": w_down.astype(MM).reshape(depth, n_chunks, FF_CHUNK, d),
        "conv_ffn": rows8(jnp.transpose(
            conv_ffn.reshape(depth, 9, n_chunks, FF_CHUNK), (0, 2, 1, 3))),
        "b_conv_ffn": b_conv_ffn.reshape(depth, n_chunks, 1, FF_CHUNK),
        "g_pre_mix": vec(g_pre_mix), "g_post_mix": vec(g_post_mix),
        "g_pre_ffn": vec(g_pre_ffn), "g_post_ffn": vec(g_post_ffn),
    }


def _tiles(seq_len):
    tile = min(MAX_TILE_ROWS, seq_len)
    assert seq_len % tile == 0 and tile % (2 * CHUNK) == 0 and tile % GRID_W == 0
    return tile


def _trunk_layer(x, mod, mrow, s0, layer, wts, *, grid_conv):
    tile = _tiles(x.shape[1])
    q, k, v, z, gb, u = _mixer_in(x, mod, mrow, layer, wts, tm=tile)
    o_f, o_b, s_fin = _deltanet(q, k, v, gb, s0, tt=tile)
    x = _mixer_out(o_f, o_b, z, u, x, mod, mrow, layer, wts, tm=tile)
    x = _ffn(x, mod, mrow, layer, wts, tm=tile, grid_conv=grid_conv)
    return x, s_fin


def kernel(x_prompt, x_sample, state_delta, c, c_ctx, w_mod, b_mod, g_pre_mix, g_post_mix,
           g_pre_ffn, g_post_ffn, w_in, conv_qkv, a_log, dt_bias, g_onorm, conv_cf, b_conv_cf,
           ln_cf_g, ln_cf_b, w_out, w_up, conv_ffn, b_conv_ffn, w_down):
    depth, d = g_pre_mix.shape
    dec_b = c.shape[0]
    assert dec_b < 8
    cc = jnp.concatenate([c, c_ctx[None, :], jnp.zeros((8 - dec_b - 1, d), c.dtype)], axis=0)
    mod = _modulation(cc, w_mod, b_mod).reshape(depth, 8, N_MOD, d)
    wts = _prepare_weights(w_in, conv_qkv, a_log, dt_bias, g_onorm, conv_cf, b_conv_cf, ln_cf_g,
                           ln_cf_b, w_out, w_up, conv_ffn, b_conv_ffn, w_down, g_pre_mix,
                           g_post_mix, g_pre_ffn, g_post_ffn)
    xp, xs = x_prompt, x_sample
    ctx_states = []
    for layer in range(depth):
        xp, s_ctx = _trunk_layer(xp, mod[layer], lambda b: dec_b, None, layer, wts,
                                 grid_conv=False)
        ctx_states.append(s_ctx)
        xs, _ = _trunk_layer(xs, mod[layer], lambda b: b, state_delta[:, layer], layer, wts,
                             grid_conv=True)
    return xp, xs, jnp.stack(ctx_states, axis=1)
```

```python
import functools

import jax
import jax.numpy as jnp
from jax import lax
from jax.experimental import pallas as pl
from jax.experimental.pallas import tpu as pltpu

F32 = jnp.float32
MM = jnp.bfloat16
EPS = 1e-6
GRID_W = 64
N_HEADS = 4
HEAD_D = 128
D_A = N_HEADS * HEAD_D
D_B = 512
SHORT_CONV = 5
CF_CONV = 31
CHUNK = 64
N_MOD = 6
HALO = 16
FF_CHUNK = 256
LANES = 128
SUBLANES = 8
SH_COLS = 512
MAX_TILE_ROWS = 512
ROWS = 64
FFN_UNROLL_MAX_ROWS = 256
FFN_CHUNKS_PER_ITER = 2
ACT_ROWS = 64
VMEM_LIMIT = 56 * 1024 * 1024


def _sigmoid(x):
    return 1.0 / (1.0 + jnp.exp(-x))


def _silu(x):
    return x * _sigmoid(x)


def _softplus(x):
    return jnp.maximum(x, 0.0) + jnp.log(1.0 + jnp.exp(-jnp.abs(x)))


def _dot(a, b):
    return jnp.dot(a.astype(MM), b.astype(MM), preferred_element_type=F32)


def _dot_nt(a, b):
    return lax.dot_general(a.astype(MM), b.astype(MM), (((1,), (1,)), ((), ())),
                           preferred_element_type=F32)


def _dot_tn(a, b):
    return lax.dot_general(a.astype(MM), b.astype(MM), (((0,), (0,)), ((), ())),
                           preferred_element_type=F32)


def _dot_exact(a, b):
    return jnp.dot(a, b, preferred_element_type=F32, precision=lax.Precision.HIGHEST)


def _rms_scale(x):
    return lax.rsqrt(jnp.mean(x * x, axis=-1, keepdims=True) + EPS)


def _mod_kernel(c_ref, w_ref, b_ref, o_ref):
    o_ref[0] = _dot_exact(_silu(c_ref[...]), w_ref[0]) + b_ref[0]


def _modulation(cc, w_mod, b_mod):
    depth, d, n = w_mod.shape
    tn = 1536
    return pl.pallas_call(
        _mod_kernel,
        grid=(depth, n // tn),
        in_specs=[
            pl.BlockSpec((8, d), lambda l, j: (0, 0)),
            pl.BlockSpec((1, d, tn), lambda l, j: (l, 0, j)),
            pl.BlockSpec((1, 1, tn), lambda l, j: (l, 0, j)),
        ],
        out_specs=pl.BlockSpec((1, 8, tn), lambda l, j: (l, 0, j)),
        out_shape=jax.ShapeDtypeStruct((depth, 8, n), F32),
        compiler_params=pltpu.CompilerParams(vmem_limit_bytes=VMEM_LIMIT),
        name="modulation",
    )(cc, w_mod, b_mod.reshape(depth, 1, n))


def _layer_spec(a, layer):
    return pl.BlockSpec((1,) + a.shape[1:], lambda b, i: (layer,) + (0,) * (a.ndim - 1))


def _mixer_in_kernel(xp_ref, xc_ref, xn_ref, mod_ref, gpre_ref, wqkv_ref, wglu_ref, wz_ref,
                     wab_ref, cq_ref, ccf_ref, bcf_ref, lng_ref, lnb_ref, alog_ref, dt_ref,
                     q_ref, k_ref, v_ref, z_ref, gb_ref, u_ref,
                     hext_ref, pext_ref, glu_ref, sh_ref, rot_ref, *, tm, nt):
    i = pl.program_id(1)
    shift = mod_ref[0, 0:1, :]
    scale1 = 1.0 + mod_ref[0, 1:2, :]
    gpre = gpre_ref[0]
    rb = ROWS
    n_ext = tm + 2 * HALO

    def norm(x):
        return ((x * _rms_scale(x)) * gpre) * scale1 + shift

    hext_ref[0:HALO, :] = jnp.where(i > 0, norm(xp_ref[0]), 0.0).astype(MM)
    hext_ref[HALO + tm:n_ext, :] = jnp.where(i < nt - 1, norm(xn_ref[0]), 0.0).astype(MM)
    for r in range(0, tm, rb):
        hext_ref[HALO + r:HALO + r + rb, :] = norm(xc_ref[0, r:r + rb, :]).astype(MM)

    hcur = hext_ref[HALO:HALO + tm, :]
    z_ref[0] = jnp.dot(hcur, wz_ref[0], preferred_element_type=F32)
    ab = jnp.dot(hcur, wab_ref[0], preferred_element_type=F32)
    g = -jnp.exp(alog_ref[0]) * _softplus(ab + dt_ref[0])
    col = lax.broadcasted_iota(jnp.int32, ab.shape, 1)
    gb_ref[0] = jnp.where(col < 2 * N_HEADS, g, _sigmoid(ab))

    pext_ref[:, 0:3 * D_A] = jnp.dot(hext_ref[...], wqkv_ref[0], preferred_element_type=F32)
    glu_ref[...] = jnp.dot(hext_ref[...], wglu_ref[0], preferred_element_type=F32)
    for r in range(0, n_ext, rb):
        rows = min(rb, n_ext - r)
        pext_ref[r:r + rows, 3 * D_A:] = (glu_ref[r:r + rows, 0:D_B]
                                          * _sigmoid(glu_ref[r:r + rows, D_B:]))

    def dwconv(c0, w_ref, wc0, width, r):
        acc = None
        for t in range(width):
            off = HALO + r + t - width // 2
            res = off % SUBLANES
            if res == 0:
                val = pext_ref[off:off + rb, c0:c0 + SH_COLS]
            else:
                val = sh_ref[res - 1, off - res:off - res + rb, :]
            term = val.reshape(rb // SUBLANES, SUBLANES, SH_COLS) * w_ref[0, t, :, wc0:wc0 + SH_COLS]
            acc = term if acc is None else acc + term
        return acc.reshape(rb, SH_COLS)

    def shifted_copies(c0, width):
        used = sorted({(HALO + t - width // 2) % SUBLANES for t in range(width)} - {0})
        groups = n_ext // SUBLANES
        x3 = pext_ref[:, c0:c0 + SH_COLS].reshape(groups, SUBLANES, SH_COLS)
        sub = lax.broadcasted_iota(jnp.int32, (groups - 1, SUBLANES, SH_COLS), 1)
        for s in used:
            rot_ref[...] = pltpu.roll(x3, SUBLANES - s, 1)
            out = jnp.where(sub < SUBLANES - s, rot_ref[0:groups - 1], rot_ref[1:groups])
            sh_ref[s - 1, :, :] = out.reshape(n_ext - SUBLANES, SH_COLS)

    for part, out_ref in enumerate((q_ref, k_ref, v_ref)):
        shifted_copies(part * D_A, SHORT_CONV)
        for r in range(0, tm, rb):
            acc = _silu(dwconv(part * D_A, cq_ref, part * D_A, SHORT_CONV, r))
            if part < 2:
                heads = []
                for h in range(N_HEADS):
                    a = acc[:, h * HEAD_D:(h + 1) * HEAD_D]
                    a = a * lax.rsqrt(jnp.sum(a * a, axis=-1, keepdims=True) + EPS)
                    heads.append(a * (HEAD_D ** -0.5) if part == 0 else a)
                acc = jnp.concatenate(heads, axis=-1)
            out_ref[0, r:r + rb, :] = acc

    shifted_copies(3 * D_A, CF_CONV)
    for r in range(0, tm, rb):
        acc = dwconv(3 * D_A, ccf_ref, 0, CF_CONV, r) + bcf_ref[0]
        mu = jnp.mean(acc, axis=-1, keepdims=True)
        cen = acc - mu
        var = jnp.mean(cen * cen, axis=-1, keepdims=True)
        y = cen * lax.rsqrt(var + EPS) * lng_ref[0] + lnb_ref[0]
        u_ref[0, r:r + rb, :] = _silu(y).astype(u_ref.dtype)


def _mixer_in(x, mod, mrow, layer, wts, *, tm):
    bsz, t, d = x.shape
    nt = t // tm
    hb = tm // HALO
    nhb = t // HALO
    names = ("g_pre_mix", "w_qkv", "w_glu", "w_z", "w_ab", "conv_qkv", "conv_cf", "b_conv_cf",
             "ln_cf_g", "ln_cf_b", "a_log", "dt_bias")
    seq = lambda c: pl.BlockSpec((1, tm, c), lambda b, i: (b, i, 0))
    return pl.pallas_call(
        functools.partial(_mixer_in_kernel, tm=tm, nt=nt),
        grid=(bsz, nt),
        in_specs=[
            pl.BlockSpec((1, HALO, d), lambda b, i: (b, jnp.maximum(i * hb - 1, 0), 0)),
            pl.BlockSpec((1, tm, d), lambda b, i: (b, i, 0)),
            pl.BlockSpec((1, HALO, d), lambda b, i: (b, jnp.minimum((i + 1) * hb, nhb - 1), 0)),
            pl.BlockSpec((1, N_MOD, d), lambda b, i: (mrow(b), 0, 0)),
        ] + [_layer_spec(wts[n], layer) for n in names],
        out_specs=[seq(D_A), seq(D_A), seq(D_A), seq(D_A), seq(LANES), seq(D_B)],
        out_shape=[
            jax.ShapeDtypeStruct((bsz, t, D_A), F32),
            jax.ShapeDtypeStruct((bsz, t, D_A), F32),
            jax.ShapeDtypeStruct((bsz, t, D_A), F32),
            jax.ShapeDtypeStruct((bsz, t, D_A), F32),
            jax.ShapeDtypeStruct((bsz, t, LANES), F32),
            jax.ShapeDtypeStruct((bsz, t, D_B), MM),
        ],
        scratch_shapes=[
            pltpu.VMEM((tm + 2 * HALO, d), MM),
            pltpu.VMEM((tm + 2 * HALO, 3 * D_A + D_B), F32),
            pltpu.VMEM((tm + 2 * HALO, 2 * D_B), F32),
            pltpu.VMEM((SUBLANES - 1, tm + 2 * HALO - SUBLANES, SH_COLS), F32),
            pltpu.VMEM(((tm + 2 * HALO) // SUBLANES, SUBLANES, SH_COLS), F32),
        ],
        compiler_params=pltpu.CompilerParams(vmem_limit_bytes=VMEM_LIMIT),
        name="mixer_in",
    )(x, x, x, mod, *[wts[n] for n in names])


def _dn_masks(d):
    n = N_HEADS * CHUNK
    ii = lax.broadcasted_iota(jnp.int32, (n, n), 0)
    jj = lax.broadcasted_iota(jnp.int32, (n, n), 1)
    lo, hi = (jj, ii) if d == 0 else (ii, jj)
    return ii, jj, lo, hi


def _dn_prep(q_ref, k_ref, v_ref, gb_ref, r0, d, p, amat_ref, tinv_ref, qk_ref, rhs_ref,
             qg_ref, kd_ref, egt_ref):
    rows = slice(r0, r0 + CHUNK)
    log2_chunk = CHUNK.bit_length() - 1
    gbc = gb_ref[0, rows, :]
    ci = lax.broadcasted_iota(jnp.int32, (CHUNK, CHUNK), 0)
    cj = lax.broadcasted_iota(jnp.int32, (CHUNK, CHUNK), 1)
    tri = (ci >= cj) if d == 0 else (ci <= cj)
    gc = _dot_exact(jnp.where(tri, 1.0, 0.0), gbc)
    last = CHUNK - 1 if d == 0 else 0
    gtot = gc[last:last + 1, :]
    egc = jnp.exp(gc)
    edk = jnp.exp(gtot - gc)
    egt_ref[p] = jnp.exp(gtot)
    gct = jnp.concatenate([gc, jnp.zeros_like(gc)], axis=0).T

    def stack_cols(a, c0):
        return jnp.concatenate([a[:, c0 + h:c0 + h + 1] for h in range(N_HEADS)], axis=0)

    c0 = d * N_HEADS
    gc_col = stack_cols(gc, c0)
    be_col = stack_cols(gbc, 2 * N_HEADS + c0)
    egc_col = stack_cols(egc, c0)
    edk_col = stack_cols(edk, c0)
    gc_row = jnp.concatenate([gct[c0 + h:c0 + h + 1, 0:CHUNK] for h in range(N_HEADS)], axis=1)

    def stack_heads(ref):
        return jnp.concatenate(
            [ref[0, rows, h * HEAD_D:(h + 1) * HEAD_D] for h in range(N_HEADS)], axis=0)

    qs, ks, vs = stack_heads(q_ref), stack_heads(k_ref), stack_heads(v_ref)
    ii, jj, lo, hi = _dn_masks(d)
    same = (ii >> log2_chunk) == (jj >> log2_chunk)
    incl = jnp.logical_and(same, lo <= hi)
    decay = jnp.where(incl, jnp.exp(jnp.where(incl, gc_col - gc_row, 0.0)), 0.0)
    kb = ks * be_col
    amat = _dot_nt(kb, ks) * decay
    amat_ref[p] = amat
    pair = jnp.logical_and((lo >> 1) == (hi >> 1), lo < hi)
    tinv_ref[p] = jnp.where(ii == jj, 1.0, 0.0) - jnp.where(pair, amat, 0.0)
    qk_ref[p] = (_dot_nt(qs, ks) * decay).astype(MM)
    rhs_ref[p] = jnp.concatenate([vs * be_col, kb * egc_col], axis=1).astype(MM)
    qg_ref[p] = (qs * egc_col).astype(MM)
    kd_ref[p] = (ks * edk_col).astype(MM)


def _deltanet_kernel(*refs, nc, zero_init):
    if zero_init:
        qf, kf, vf, gf, qb, kb, vb, gb, _, of_ref, ob_ref, sfin_ref = refs[:12]
    else:
        qf, kf, vf, gf, qb, kb, vb, gb, s0_ref, of_ref, ob_ref, sfin_ref = refs[:12]
    amat_ref, tinv_ref, qk_ref, rhs_ref, qg_ref, kd_ref, egt_ref, uw_ref, s_ref = refs[-9:]
    j = pl.program_id(1)
    log2_chunk = CHUNK.bit_length() - 1
    n = N_HEADS * CHUNK

    @pl.when(j == 0)
    def _():
        if zero_init:
            s_ref[...] = jnp.zeros_like(s_ref)
        else:
            s_ref[...] = s0_ref[0]

    probs = []
    for c in range(nc):
        probs.append((0, c * CHUNK, (qf, kf, vf, gf), of_ref))
        probs.append((1, (nc - 1 - c) * CHUNK, (qb, kb, vb, gb), ob_ref))

    for p, (d, r0, (q_ref, k_ref, v_ref, g_ref), _) in enumerate(probs):
        _dn_prep(q_ref, k_ref, v_ref, g_ref, r0, d, p, amat_ref, tinv_ref, qk_ref, rhs_ref,
                 qg_ref, kd_ref, egt_ref)

    for lb in range(1, log2_chunk):
        e_masks = []
        for d in (0, 1):
            _, _, lo, hi = _dn_masks(d)
            e_masks.append(jnp.logical_and((lo >> (lb + 1)) == (hi >> (lb + 1)),
                                           (lo >> lb) < (hi >> lb)))
        b = 1 << lb
        for p, (d, _, _, _) in enumerate(probs):
            tb = tinv_ref[p].astype(MM)
            e = jnp.where(e_masks[d], amat_ref[p], 0.0)
            if b < SUBLANES:
                tinv_ref[p] = tinv_ref[p] - _dot(_dot(tb, e), tb)
            else:
                starts = [s + (b if d == 0 else 0) for s in range(0, n, 2 * b)]
                sel = jnp.concatenate([tinv_ref[p, s:s + b, :] for s in starts], axis=0)
                x = _dot(_dot(sel, e), tb)
                for idx, s in enumerate(starts):
                    tinv_ref[p, s:s + b, :] = sel[idx * b:(idx + 1) * b] - x[idx * b:(idx + 1) * b]

    for p in range(len(probs)):
        uw_ref[p] = _dot(tinv_ref[p], rhs_ref[p])

    for p, (d, r0, _, o_ref) in enumerate(probs):
        c0 = d * N_HEADS
        vnew, o1 = [], []
        for h in range(N_HEADS):
            hs = slice(h * CHUNK, (h + 1) * CHUNK)
            wq = jnp.concatenate([uw_ref[p, hs, HEAD_D:].astype(MM), qg_ref[p, hs, :]], axis=0)
            r = _dot(wq, s_ref[d, h])
            vnew.append(uw_ref[p, hs, 0:HEAD_D] - r[:CHUNK])
            o1.append(r[CHUNK:])
        vnew = jnp.concatenate(vnew, axis=0).astype(MM)
        o = jnp.concatenate(o1, axis=0) + _dot(qk_ref[p], vnew)
        for h in range(N_HEADS):
            hs = slice(h * CHUNK, (h + 1) * CHUNK)
            s_ref[d, h] = (s_ref[d, h] * egt_ref[p, :, c0 + h:c0 + h + 1]
                           + _dot_tn(kd_ref[p, hs, :], vnew[hs]))
            o_ref[0, r0:r0 + CHUNK, h * HEAD_D:(h + 1) * HEAD_D] = o[hs]

    @pl.when(j == pl.num_programs(1) - 1)
    def _():
        if zero_init:
            sfin_ref[0, 0] = s_ref[...]
        else:
            sfin_ref[0] = s_ref[...]


def _deltanet(q, k, v, gb, s0, states, layer, *, tt):
    bsz, t, _ = q.shape
    nt = t // tt
    nc = tt // CHUNK
    n = N_HEADS * CHUNK
    zero_init = s0 is None
    fwd = lambda c: pl.BlockSpec((1, tt, c), lambda b, j: (b, j, 0))
    bwd = lambda c: pl.BlockSpec((1, tt, c), lambda b, j: (b, nt - 1 - j, 0))
    st_block = (2, N_HEADS, HEAD_D, HEAD_D)
    in_specs = [fwd(D_A), fwd(D_A), fwd(D_A), fwd(LANES), bwd(D_A), bwd(D_A), bwd(D_A), bwd(LANES)]
    args = [q, k, v, gb, q, k, v, gb]
    if zero_init:
        in_specs.append(pl.BlockSpec(memory_space=pl.ANY))
        args.append(states)
        st_out_spec = pl.BlockSpec((1, 1) + st_block, lambda b, j: (b, layer, 0, 0, 0, 0))
        st_out_shape = jax.ShapeDtypeStruct(states.shape, F32)
        aliases = {len(args) - 1: 2}
    else:
        st_spec = pl.BlockSpec((1,) + st_block, lambda b, j: (b, 0, 0, 0, 0))
        in_specs.append(st_spec)
        args.append(s0)
        st_out_spec = st_spec
        st_out_shape = jax.ShapeDtypeStruct((bsz,) + st_block, F32)
        aliases = {}
    n_prob = 2 * nc
    return pl.pallas_call(
        functools.partial(_deltanet_kernel, nc=nc, zero_init=zero_init),
        grid=(bsz, nt),
        in_specs=in_specs,
        out_specs=[fwd(D_A), bwd(D_A), st_out_spec],
        out_shape=[
            jax.ShapeDtypeStruct((bsz, t, D_A), F32),
            jax.ShapeDtypeStruct((bsz, t, D_A), F32),
            st_out_shape,
        ],
        input_output_aliases=aliases,
        scratch_shapes=[
            pltpu.VMEM((n_prob, n, n), F32),
            pltpu.VMEM((n_prob, n, n), F32),
            pltpu.VMEM((n_prob, n, n), MM),
            pltpu.VMEM((n_prob, n, 2 * HEAD_D), MM),
            pltpu.VMEM((n_prob, n, HEAD_D), MM),
            pltpu.VMEM((n_prob, n, HEAD_D), MM),
            pltpu.VMEM((n_prob, 1, LANES), F32),
            pltpu.VMEM((n_prob, n, 2 * HEAD_D), F32),
            pltpu.VMEM((2, N_HEADS, HEAD_D, HEAD_D), F32),
        ],
        compiler_params=pltpu.CompilerParams(
            dimension_semantics=("parallel", "arbitrary"), vmem_limit_bytes=VMEM_LIMIT),
        name="deltanet",
    )(*args)


def _mixer_out_kernel(of_ref, ob_ref, z_ref, u_ref, x_ref, mod_ref, gon_ref, wo_ref, gpost_ref,
                      out_ref, cat_ref, mix_ref, *, tm):
    rb = ROWS
    gate = mod_ref[0, 2:3, :]
    for r in range(0, tm, rb):
        rows = slice(r, r + rb)
        o = of_ref[0, rows, :] + ob_ref[0, rows, :]
        z = z_ref[0, rows, :]
        for h in range(N_HEADS):
            hs = slice(h * HEAD_D, (h + 1) * HEAD_D)
            oh = o[:, hs]
            oh = oh * _rms_scale(oh) * gon_ref[0]
            cat_ref[rows, hs] = (oh * _silu(z[:, hs])).astype(MM)
        cat_ref[rows, D_A:] = u_ref[0, rows, :]
    mix_ref[...] = jnp.dot(cat_ref[...], wo_ref[0], preferred_element_type=F32)
    for r in range(0, tm, rb):
        rows = slice(r, r + rb)
        mix = mix_ref[rows, :]
        mix = mix * _rms_scale(mix) * gpost_ref[0]
        out_ref[0, rows, :] = x_ref[0, rows, :] + gate * mix


def _mixer_out(o_f, o_b, z, u, x, mod, mrow, layer, wts, *, tm):
    bsz, t, d = x.shape
    names = ("g_onorm", "w_out", "g_post_mix")
    seq = lambda c: pl.BlockSpec((1, tm, c), lambda b, i: (b, i, 0))
    return pl.pallas_call(
        functools.partial(_mixer_out_kernel, tm=tm),
        grid=(bsz, t // tm),
        in_specs=[
            seq(D_A), seq(D_A), seq(D_A), seq(D_B), seq(d),
            pl.BlockSpec((1, N_MOD, d), lambda b, i: (mrow(b), 0, 0)),
        ] + [_layer_spec(wts[n], layer) for n in names],
        out_specs=seq(d),
        out_shape=jax.ShapeDtypeStruct((bsz, t, d), F32),
        scratch_shapes=[pltpu.VMEM((tm, D_A + D_B), MM), pltpu.VMEM((tm, d), F32)],
        compiler_params=pltpu.CompilerParams(vmem_limit_bytes=VMEM_LIMIT),
        name="mixer_out",
    )(o_f, o_b, z, u, x, mod, *[wts[n] for n in names])


def _ffn_kernel(xp_ref, xc_ref, xn_ref, mod_ref, gpre_ref, wgu_ref, wd_ref, cw_ref,
                cb_ref, gpost_ref, out_ref, hext_ref, gts0_ref, gts1_ref, up0_ref, up1_ref,
                act0_ref, act1_ref, y_ref, *, tm, nt, halo, width, taps, n_chunks, unroll_all):
    i = pl.program_id(1)
    pad = 8
    shift = mod_ref[0, 3:4, :]
    scale1 = 1.0 + mod_ref[0, 4:5, :]
    gate = mod_ref[0, 5:6, :]
    gpre = gpre_ref[0]
    rb = ROWS
    n_ext = tm + 2 * halo

    def norm(x):
        return ((x * _rms_scale(x)) * gpre) * scale1 + shift

    if halo:
        hext_ref[0:halo, :] = jnp.where(i > 0, norm(xp_ref[0]), 0.0).astype(MM)
        hext_ref[halo + tm:n_ext, :] = jnp.where(i < nt - 1, norm(xn_ref[0]), 0.0).astype(MM)
    for r in range(0, tm, rb):
        hext_ref[halo + r:halo + r + rb, :] = norm(xc_ref[0, r:r + rb, :]).astype(MM)

    gts, ups, acts = (gts0_ref, gts1_ref), (up0_ref, up1_ref), (act0_ref, act1_ref)
    for g_ref in gts:
        g_ref[0:pad, :] = jnp.zeros((pad, FF_CHUNK), F32)
        g_ref[pad + n_ext:pad + n_ext + pad, :] = jnp.zeros((pad, FF_CHUNK), F32)
    y_ref[...] = jnp.zeros_like(y_ref)

    def project(c, slot):
        gts[slot][pad:pad + n_ext, :] = jnp.dot(hext_ref[...], wgu_ref[0, c],
                                                preferred_element_type=F32)
        ups[slot][...] = jnp.dot(hext_ref[halo:halo + tm, :], wgu_ref[0, n_chunks + c],
                                 preferred_element_type=F32)

    def activate(c, slot, rb=ACT_ROWS):
        for r in range(0, tm, rb):
            pos = (r + lax.broadcasted_iota(jnp.int32, (rb, 1), 0)) % width
            acc = None
            for dr, dc, widx in taps:
                s = pad + halo + r + dr * GRID_W + dc
                val = gts[slot][s:s + rb, :]
                if dc != 0:
                    ok = (pos >= 1) if dc < 0 else (pos <= width - 2)
                    val = jnp.where(ok, val, 0.0)
                term = val.reshape(rb // SUBLANES, SUBLANES, FF_CHUNK) * cw_ref[0, c, widx]
                acc = term if acc is None else acc + term
            gt = acc.reshape(rb, FF_CHUNK) + cb_ref[0, c]
            acts[slot][r:r + rb, :] = (_silu(gt) * ups[slot][r:r + rb, :]).astype(MM)

    def contract(c, slot):
        y_ref[...] += jnp.dot(acts[slot][...], wd_ref[0, c], preferred_element_type=F32)

    def steady(c, slot):
        activate(c, slot)
        project(c + 1, 1 - slot)
        contract(c, slot)

    project(0, 0)
    n_steady = n_chunks - 1
    if unroll_all:
        n_looped = 0
    else:
        per_iter = FFN_CHUNKS_PER_ITER
        n_looped = n_steady - n_steady % per_iter

        def loop_body(k, carry):
            for s in range(per_iter):
                steady(per_iter * k + s, s % 2)
            return carry

        lax.fori_loop(0, n_looped // per_iter, loop_body, 0)
    for c in range(n_looped, n_steady):
        steady(c, c % 2)
    last = n_chunks - 1
    activate(last, last % 2)
    contract(last, last % 2)

    for r in range(0, tm, rb):
        y = y_ref[r:r + rb, :]
        y = y * _rms_scale(y) * gpost_ref[0]
        out_ref[0, r:r + rb, :] = xc_ref[0, r:r + rb, :] + gate * y


def _ffn(x, mod, mrow, layer, wts, *, tm, grid_conv):
    bsz, t, d = x.shape
    nt = t // tm
    n_chunks = wts["w_down"].shape[1]
    if grid_conv:
        halo = GRID_W
        width = GRID_W
        taps = tuple((dr, dc, (dr + 1) * 3 + (dc + 1)) for dr in (-1, 0, 1) for dc in (-1, 0, 1))
        hb, nhb = tm // halo, t // halo
        xp_spec = pl.BlockSpec((1, halo, d), lambda b, i: (b, jnp.maximum(i * hb - 1, 0), 0))
        xn_spec = pl.BlockSpec((1, halo, d), lambda b, i: (b, jnp.minimum((i + 1) * hb, nhb - 1), 0))
    else:
        assert tm == t
        halo = 0
        width = t
        taps = tuple((0, dc, 3 + (dc + 1)) for dc in (-1, 0, 1))
        xp_spec = pl.BlockSpec((1, 8, d), lambda b, i: (b, 0, 0))
        xn_spec = pl.BlockSpec((1, 8, d), lambda b, i: (b, 0, 0))
    names = ("g_pre_ffn", "w_gu", "w_down", "conv_ffn", "b_conv_ffn", "g_post_ffn")
    return pl.pallas_call(
        functools.partial(_ffn_kernel, tm=tm, nt=nt, halo=halo, width=width, taps=taps,
                          n_chunks=n_chunks, unroll_all=tm <= FFN_UNROLL_MAX_ROWS),
        grid=(bsz, nt),
        in_specs=[
            xp_spec,
            pl.BlockSpec((1, tm, d), lambda b, i: (b, i, 0)),
            xn_spec,
            pl.BlockSpec((1, N_MOD, d), lambda b, i: (mrow(b), 0, 0)),
        ] + [_layer_spec(wts[n], layer) for n in names],
        out_specs=pl.BlockSpec((1, tm, d), lambda b, i: (b, i, 0)),
        out_shape=jax.ShapeDtypeStruct((bsz, t, d), F32),
        scratch_shapes=[
            pltpu.VMEM((tm + 2 * halo, d), MM),
            pltpu.VMEM((tm + 2 * halo + 16, FF_CHUNK), F32),
            pltpu.VMEM((tm + 2 * halo + 16, FF_CHUNK), F32),
            pltpu.VMEM((tm, FF_CHUNK), F32),
            pltpu.VMEM((tm, FF_CHUNK), F32),
            pltpu.VMEM((tm, FF_CHUNK), MM),
            pltpu.VMEM((tm, FF_CHUNK), MM),
            pltpu.VMEM((tm, d), F32),
        ],
        compiler_params=pltpu.CompilerParams(vmem_limit_bytes=VMEM_LIMIT),
        name="ffn",
    )(x, x, x, mod, *[wts[n] for n in names])


def _prepare_weights(w_in, conv_qkv, a_log, dt_bias, g_onorm, conv_cf, b_conv_cf, ln_cf_g,
                     ln_cf_b, w_out, w_up, conv_ffn, b_conv_ffn, w_down, g_pre_mix, g_post_mix,
                     g_pre_ffn, g_post_ffn):
    depth, d, _ = w_in.shape
    d_ff = w_down.shape[1]
    n_chunks = d_ff // FF_CHUNK
    n_ab = 4 * N_HEADS
    w_in = w_in.astype(MM)
    w_up = w_up.astype(MM)
    ab = w_in[:, :, 4 * D_A:4 * D_A + n_ab]
    vec = lambda a: a.reshape(depth, 1, -1)
    rows8 = lambda a: jnp.broadcast_to(a[..., None, :], a.shape[:-1] + (SUBLANES, a.shape[-1]))
    pad_lane = lambda a: jnp.pad(a.reshape(depth, 1, -1),
                                 ((0, 0), (0, 0), (0, LANES - 2 * N_HEADS)))
    return {
        "w_qkv": w_in[:, :, :3 * D_A],
        "w_z": w_in[:, :, 3 * D_A:4 * D_A],
        "w_ab": jnp.pad(ab, ((0, 0), (0, 0), (0, LANES - n_ab))),
        "w_glu": w_in[:, :, 4 * D_A + n_ab:],
        "conv_qkv": rows8(conv_qkv), "conv_cf": rows8(conv_cf), "b_conv_cf": vec(b_conv_cf),
        "ln_cf_g": vec(ln_cf_g), "ln_cf_b": vec(ln_cf_b),
        "a_log": pad_lane(a_log), "dt_bias": pad_lane(dt_bias),
        "g_onorm": vec(g_onorm), "w_out": w_out.astype(MM),
        "w_gu": jnp.transpose(w_up.reshape(depth, d, 2 * n_chunks, FF_CHUNK), (0, 2, 1, 3)),
        "w_down": w_down.astype(MM).reshape(depth, n_chunks, FF_CHUNK, d),
        "conv_ffn": rows8(jnp.transpose(
            conv_ffn.reshape(depth, 9, n_chunks, FF_CHUNK), (0, 2, 1, 3))),
        "b_conv_ffn": b_conv_ffn.reshape(depth, n_chunks, 1, FF_CHUNK),
        "g_pre_mix": vec(g_pre_mix), "g_post_mix": vec(g_post_mix),
        "g_pre_ffn": vec(g_pre_ffn), "g_post_ffn": vec(g_post_ffn),
    }


def _tiles(seq_len):
    tile = min(MAX_TILE_ROWS, seq_len)
    assert seq_len % tile == 0 and tile % (2 * CHUNK) == 0 and tile % GRID_W == 0
    return tile


def _trunk_layer(x, mod, mrow, s0, states, layer, wts, *, grid_conv):
    tile = _tiles(x.shape[1])
    q, k, v, z, gb, u = _mixer_in(x, mod, mrow, layer, wts, tm=tile)
    o_f, o_b, s_fin = _deltanet(q, k, v, gb, s0, states, layer, tt=tile)
    x = _mixer_out(o_f, o_b, z, u, x, mod, mrow, layer, wts, tm=tile)
    x = _ffn(x, mod, mrow, layer, wts, tm=tile, grid_conv=grid_conv)
    return x, s_fin


def kernel(x_prompt, x_sample, state_delta, c, c_ctx, w_mod, b_mod, g_pre_mix, g_post_mix,
           g_pre_ffn, g_post_ffn, w_in, conv_qkv, a_log, dt_bias, g_onorm, conv_cf, b_conv_cf,
           ln_cf_g, ln_cf_b, w_out, w_up, conv_ffn, b_conv_ffn, w_down):
    depth, d = g_pre_mix.shape
    dec_b = c.shape[0]
    assert dec_b < 8
    cc = jnp.concatenate([c, c_ctx[None, :], jnp.zeros((8 - dec_b - 1, d), c.dtype)], axis=0)
    mod = _modulation(cc, w_mod, b_mod).reshape(depth, 8, N_MOD, d)
    wts = _prepare_weights(w_in, conv_qkv, a_log, dt_bias, g_onorm, conv_cf, b_conv_cf, ln_cf_g,
                           ln_cf_b, w_out, w_up, conv_ffn, b_conv_ffn, w_down, g_pre_mix,
                           g_post_mix, g_pre_ffn, g_post_ffn)
    xp, xs = x_prompt, x_sample
    ctx_states = jnp.zeros((xp.shape[0], depth, 2, N_HEADS, HEAD_D, HEAD_D), F32)
    for layer in range(depth):
        xp, ctx_states = _trunk_layer(xp, mod[layer], lambda b: dec_b, None, ctx_states, layer,
                                      wts, grid_conv=False)
        xs, _ = _trunk_layer(xs, mod[layer], lambda b: b, state_delta[:, layer], None, layer,
                             wts, grid_conv=True)
    return xp, xs, ctx_states
```

```python
import functools

import jax
import jax.numpy as jnp
from jax import lax
from jax.experimental import pallas as pl
from jax.experimental.pallas import tpu as pltpu

F32 = jnp.float32
MM = jnp.bfloat16
EPS = 1e-6
GRID_W = 64
N_HEADS = 4
HEAD_D = 128
D_A = N_HEADS * HEAD_D
D_B = 512
SHORT_CONV = 5
CF_CONV = 31
CHUNK = 64
N_MOD = 6
HALO = 16
FF_CHUNK = 256
LANES = 128
SUBLANES = 8
SH_COLS = 512
MAX_TILE_ROWS = 512
ROWS = 64
FFN_UNROLL_MAX_ROWS = 256
FFN_CHUNKS_PER_ITER = 2
ACT_ROWS = 64
MOD_COLS = 1536
VMEM_LIMIT = 56 * 1024 * 1024


def _sigmoid(x):
    return 1.0 / (1.0 + jnp.exp(-x))


def _silu(x):
    return x * _sigmoid(x)


def _softplus(x):
    return jnp.maximum(x, 0.0) + jnp.log(1.0 + jnp.exp(-jnp.abs(x)))


def _dot(a, b):
    return jnp.dot(a.astype(MM), b.astype(MM), preferred_element_type=F32)


def _dot_nt(a, b):
    return lax.dot_general(a.astype(MM), b.astype(MM), (((1,), (1,)), ((), ())),
                           preferred_element_type=F32)


def _dot_tn(a, b):
    return lax.dot_general(a.astype(MM), b.astype(MM), (((0,), (0,)), ((), ())),
                           preferred_element_type=F32)


def _dot_exact(a, b):
    return jnp.dot(a, b, preferred_element_type=F32, precision=lax.Precision.HIGHEST)


def _rms_scale(x):
    return lax.rsqrt(jnp.mean(x * x, axis=-1, keepdims=True) + EPS)


def _mod_kernel(c_ref, w_ref, b_ref, o_ref):
    o_ref[0] = _dot_exact(_silu(c_ref[...]), w_ref[0]) + b_ref[0]


def _modulation(cc, w_mod, b_mod):
    depth, d, n = w_mod.shape
    tn = MOD_COLS
    return pl.pallas_call(
        _mod_kernel,
        grid=(depth, n // tn),
        in_specs=[
            pl.BlockSpec((8, d), lambda l, j: (0, 0)),
            pl.BlockSpec((1, d, tn), lambda l, j: (l, 0, j)),
            pl.BlockSpec((1, 1, tn), lambda l, j: (l, 0, j)),
        ],
        out_specs=pl.BlockSpec((1, 8, tn), lambda l, j: (l, 0, j)),
        out_shape=jax.ShapeDtypeStruct((depth, 8, n), F32),
        compiler_params=pltpu.CompilerParams(vmem_limit_bytes=VMEM_LIMIT),
        name="modulation",
    )(cc, w_mod, b_mod.reshape(depth, 1, n))


def _layer_spec(a, layer):
    return pl.BlockSpec((1,) + a.shape[1:], lambda b, i: (layer,) + (0,) * (a.ndim - 1))


def _mixer_in_kernel(xp_ref, xc_ref, xn_ref, mod_ref, gpre_ref, wqkv_ref, wglu_ref, wz_ref,
                     wab_ref, cq_ref, ccf_ref, bcf_ref, lng_ref, lnb_ref, alog_ref, dt_ref,
                     q_ref, k_ref, v_ref, z_ref, gb_ref, u_ref,
                     hext_ref, pext_ref, glu_ref, sh_ref, rot_ref, *, tm, nt):
    i = pl.program_id(1)
    shift = mod_ref[0, 0:1, :]
    scale1 = 1.0 + mod_ref[0, 1:2, :]
    gpre = gpre_ref[0]
    rb = ROWS
    n_ext = tm + 2 * HALO

    def norm(x):
        return ((x * _rms_scale(x)) * gpre) * scale1 + shift

    hext_ref[0:HALO, :] = jnp.where(i > 0, norm(xp_ref[0]), 0.0).astype(MM)
    hext_ref[HALO + tm:n_ext, :] = jnp.where(i < nt - 1, norm(xn_ref[0]), 0.0).astype(MM)
    for r in range(0, tm, rb):
        hext_ref[HALO + r:HALO + r + rb, :] = norm(xc_ref[0, r:r + rb, :]).astype(MM)

    hcur = hext_ref[HALO:HALO + tm, :]
    z_ref[0] = jnp.dot(hcur, wz_ref[0], preferred_element_type=F32).astype(z_ref.dtype)
    ab = jnp.dot(hcur, wab_ref[0], preferred_element_type=F32)
    g = -jnp.exp(alog_ref[0]) * _softplus(ab + dt_ref[0])
    col = lax.broadcasted_iota(jnp.int32, ab.shape, 1)
    gb_ref[0] = jnp.where(col < 2 * N_HEADS, g, _sigmoid(ab))

    pext_ref[:, 0:3 * D_A] = jnp.dot(hext_ref[...], wqkv_ref[0], preferred_element_type=F32)
    glu_ref[...] = jnp.dot(hext_ref[...], wglu_ref[0], preferred_element_type=F32)
    for r in range(0, n_ext, rb):
        rows = min(rb, n_ext - r)
        pext_ref[r:r + rows, 3 * D_A:] = (glu_ref[r:r + rows, 0:D_B]
                                          * _sigmoid(glu_ref[r:r + rows, D_B:]))

    def dwconv(c0, w_ref, wc0, width, r):
        acc = None
        for t in range(width):
            off = HALO + r + t - width // 2
            res = off % SUBLANES
            if res == 0:
                val = pext_ref[off:off + rb, c0:c0 + SH_COLS]
            else:
                val = sh_ref[res - 1, off - res:off - res + rb, :]
            term = val.reshape(rb // SUBLANES, SUBLANES, SH_COLS) * w_ref[0, t, :, wc0:wc0 + SH_COLS]
            acc = term if acc is None else acc + term
        return acc.reshape(rb, SH_COLS)

    def shifted_copies(c0, width):
        used = sorted({(HALO + t - width // 2) % SUBLANES for t in range(width)} - {0})
        groups = n_ext // SUBLANES
        x3 = pext_ref[:, c0:c0 + SH_COLS].reshape(groups, SUBLANES, SH_COLS)
        sub = lax.broadcasted_iota(jnp.int32, (groups - 1, SUBLANES, SH_COLS), 1)
        for s in used:
            rot_ref[...] = pltpu.roll(x3, SUBLANES - s, 1)
            out = jnp.where(sub < SUBLANES - s, rot_ref[0:groups - 1], rot_ref[1:groups])
            sh_ref[s - 1, :, :] = out.reshape(n_ext - SUBLANES, SH_COLS)

    for part, out_ref in enumerate((q_ref, k_ref, v_ref)):
        shifted_copies(part * D_A, SHORT_CONV)
        for r in range(0, tm, rb):
            acc = _silu(dwconv(part * D_A, cq_ref, part * D_A, SHORT_CONV, r))
            if part < 2:
                heads = []
                for h in range(N_HEADS):
                    a = acc[:, h * HEAD_D:(h + 1) * HEAD_D]
                    a = a * lax.rsqrt(jnp.sum(a * a, axis=-1, keepdims=True) + EPS)
                    heads.append(a * (HEAD_D ** -0.5) if part == 0 else a)
                acc = jnp.concatenate(heads, axis=-1)
            out_ref[0, r:r + rb, :] = acc

    shifted_copies(3 * D_A, CF_CONV)
    for r in range(0, tm, rb):
        acc = dwconv(3 * D_A, ccf_ref, 0, CF_CONV, r) + bcf_ref[0]
        mu = jnp.mean(acc, axis=-1, keepdims=True)
        cen = acc - mu
        var = jnp.mean(cen * cen, axis=-1, keepdims=True)
        y = cen * lax.rsqrt(var + EPS) * lng_ref[0] + lnb_ref[0]
        u_ref[0, r:r + rb, :] = _silu(y).astype(u_ref.dtype)


def _mixer_in(x, mod, mrow, layer, wts, *, tm):
    bsz, t, d = x.shape
    nt = t // tm
    hb = tm // HALO
    nhb = t // HALO
    names = ("g_pre_mix", "w_qkv", "w_glu", "w_z", "w_ab", "conv_qkv", "conv_cf", "b_conv_cf",
             "ln_cf_g", "ln_cf_b", "a_log", "dt_bias")
    seq = lambda c: pl.BlockSpec((1, tm, c), lambda b, i: (b, i, 0))
    return pl.pallas_call(
        functools.partial(_mixer_in_kernel, tm=tm, nt=nt),
        grid=(bsz, nt),
        in_specs=[
            pl.BlockSpec((1, HALO, d), lambda b, i: (b, jnp.maximum(i * hb - 1, 0), 0)),
            pl.BlockSpec((1, tm, d), lambda b, i: (b, i, 0)),
            pl.BlockSpec((1, HALO, d), lambda b, i: (b, jnp.minimum((i + 1) * hb, nhb - 1), 0)),
            pl.BlockSpec((1, N_MOD, d), lambda b, i: (mrow(b), 0, 0)),
        ] + [_layer_spec(wts[n], layer) for n in names],
        out_specs=[seq(D_A), seq(D_A), seq(D_A), seq(D_A), seq(LANES), seq(D_B)],
        out_shape=[
            jax.ShapeDtypeStruct((bsz, t, D_A), F32),
            jax.ShapeDtypeStruct((bsz, t, D_A), F32),
            jax.ShapeDtypeStruct((bsz, t, D_A), F32),
            jax.ShapeDtypeStruct((bsz, t, D_A), MM),
            jax.ShapeDtypeStruct((bsz, t, LANES), F32),
            jax.ShapeDtypeStruct((bsz, t, D_B), MM),
        ],
        scratch_shapes=[
            pltpu.VMEM((tm + 2 * HALO, d), MM),
            pltpu.VMEM((tm + 2 * HALO, 3 * D_A + D_B), F32),
            pltpu.VMEM((tm + 2 * HALO, 2 * D_B), F32),
            pltpu.VMEM((SUBLANES - 1, tm + 2 * HALO - SUBLANES, SH_COLS), F32),
            pltpu.VMEM(((tm + 2 * HALO) // SUBLANES, SUBLANES, SH_COLS), F32),
        ],
        compiler_params=pltpu.CompilerParams(vmem_limit_bytes=VMEM_LIMIT),
        name="mixer_in",
    )(x, x, x, mod, *[wts[n] for n in names])


def _dn_masks(d):
    n = N_HEADS * CHUNK
    ii = lax.broadcasted_iota(jnp.int32, (n, n), 0)
    jj = lax.broadcasted_iota(jnp.int32, (n, n), 1)
    lo, hi = (jj, ii) if d == 0 else (ii, jj)
    return ii, jj, lo, hi


def _dn_prep(q_ref, k_ref, v_ref, gb_ref, r0, d, p, amat_ref, tinv_ref, qk_ref, rhs_ref,
             qg_ref, kd_ref, egt_ref):
    rows = slice(r0, r0 + CHUNK)
    log2_chunk = CHUNK.bit_length() - 1
    gbc = gb_ref[0, rows, :]
    ci = lax.broadcasted_iota(jnp.int32, (CHUNK, CHUNK), 0)
    cj = lax.broadcasted_iota(jnp.int32, (CHUNK, CHUNK), 1)
    tri = (ci >= cj) if d == 0 else (ci <= cj)
    gc = _dot_exact(jnp.where(tri, 1.0, 0.0), gbc)
    last = CHUNK - 1 if d == 0 else 0
    gtot = gc[last:last + 1, :]
    egc = jnp.exp(gc)
    edk = jnp.exp(gtot - gc)
    egt_ref[p] = jnp.exp(gtot)
    gct = jnp.concatenate([gc, jnp.zeros_like(gc)], axis=0).T

    def stack_cols(a, c0):
        return jnp.concatenate([a[:, c0 + h:c0 + h + 1] for h in range(N_HEADS)], axis=0)

    c0 = d * N_HEADS
    gc_col = stack_cols(gc, c0)
    be_col = stack_cols(gbc, 2 * N_HEADS + c0)
    egc_col = stack_cols(egc, c0)
    edk_col = stack_cols(edk, c0)
    gc_row = jnp.concatenate([gct[c0 + h:c0 + h + 1, 0:CHUNK] for h in range(N_HEADS)], axis=1)

    def stack_heads(ref):
        return jnp.concatenate(
            [ref[0, rows, h * HEAD_D:(h + 1) * HEAD_D] for h in range(N_HEADS)], axis=0)

    qs, ks, vs = stack_heads(q_ref), stack_heads(k_ref), stack_heads(v_ref)
    ii, jj, lo, hi = _dn_masks(d)
    same = (ii >> log2_chunk) == (jj >> log2_chunk)
    incl = jnp.logical_and(same, lo <= hi)
    decay = jnp.where(incl, jnp.exp(jnp.where(incl, gc_col - gc_row, 0.0)), 0.0)
    kb = ks * be_col
    amat = _dot_nt(kb, ks) * decay
    amat_ref[p] = amat
    pair = jnp.logical_and((lo >> 1) == (hi >> 1), lo < hi)
    tinv_ref[p] = jnp.where(ii == jj, 1.0, 0.0) - jnp.where(pair, amat, 0.0)
    qk_ref[p] = (_dot_nt(qs, ks) * decay).astype(MM)
    rhs_ref[p] = jnp.concatenate([vs * be_col, kb * egc_col], axis=1).astype(MM)
    qg_ref[p] = (qs * egc_col).astype(MM)
    kd_ref[p] = (ks * edk_col).astype(MM)


def _deltanet_kernel(*refs, nc, zero_init):
    if zero_init:
        qf, kf, vf, gf, qb, kb, vb, gb, _, of_ref, ob_ref, sfin_ref = refs[:12]
    else:
        qf, kf, vf, gf, qb, kb, vb, gb, s0_ref, of_ref, ob_ref, sfin_ref = refs[:12]
    amat_ref, tinv_ref, qk_ref, rhs_ref, qg_ref, kd_ref, egt_ref, uw_ref, s_ref = refs[-9:]
    j = pl.program_id(1)
    log2_chunk = CHUNK.bit_length() - 1
    n = N_HEADS * CHUNK

    @pl.when(j == 0)
    def _():
        if zero_init:
            s_ref[...] = jnp.zeros_like(s_ref)
        else:
            s_ref[...] = s0_ref[0]

    probs = []
    for c in range(nc):
        probs.append((0, c * CHUNK, (qf, kf, vf, gf), of_ref))
        probs.append((1, (nc - 1 - c) * CHUNK, (qb, kb, vb, gb), ob_ref))

    for p, (d, r0, (q_ref, k_ref, v_ref, g_ref), _) in enumerate(probs):
        _dn_prep(q_ref, k_ref, v_ref, g_ref, r0, d, p, amat_ref, tinv_ref, qk_ref, rhs_ref,
                 qg_ref, kd_ref, egt_ref)

    for lb in range(1, log2_chunk):
        e_masks = []
        for d in (0, 1):
            _, _, lo, hi = _dn_masks(d)
            e_masks.append(jnp.logical_and((lo >> (lb + 1)) == (hi >> (lb + 1)),
                                           (lo >> lb) < (hi >> lb)))
        b = 1 << lb
        for p, (d, _, _, _) in enumerate(probs):
            tb = tinv_ref[p].astype(MM)
            e = jnp.where(e_masks[d], amat_ref[p], 0.0)
            if b < SUBLANES:
                tinv_ref[p] = tinv_ref[p] - _dot(_dot(tb, e), tb)
            else:
                starts = [s + (b if d == 0 else 0) for s in range(0, n, 2 * b)]
                sel = jnp.concatenate([tinv_ref[p, s:s + b, :] for s in starts], axis=0)
                x = _dot(_dot(sel, e), tb)
                for idx, s in enumerate(starts):
                    tinv_ref[p, s:s + b, :] = sel[idx * b:(idx + 1) * b] - x[idx * b:(idx + 1) * b]

    for p in range(len(probs)):
        uw_ref[p] = _dot(tinv_ref[p], rhs_ref[p])

    for p, (d, r0, _, o_ref) in enumerate(probs):
        c0 = d * N_HEADS
        vnew, o1 = [], []
        for h in range(N_HEADS):
            hs = slice(h * CHUNK, (h + 1) * CHUNK)
            wq = jnp.concatenate([uw_ref[p, hs, HEAD_D:].astype(MM), qg_ref[p, hs, :]], axis=0)
            r = _dot(wq, s_ref[d, h])
            vnew.append(uw_ref[p, hs, 0:HEAD_D] - r[:CHUNK])
            o1.append(r[CHUNK:])
        vnew = jnp.concatenate(vnew, axis=0).astype(MM)
        o = jnp.concatenate(o1, axis=0) + _dot(qk_ref[p], vnew)
        for h in range(N_HEADS):
            hs = slice(h * CHUNK, (h + 1) * CHUNK)
            s_ref[d, h] = (s_ref[d, h] * egt_ref[p, :, c0 + h:c0 + h + 1]
                           + _dot_tn(kd_ref[p, hs, :], vnew[hs]))
            o_ref[0, r0:r0 + CHUNK, h * HEAD_D:(h + 1) * HEAD_D] = o[hs].astype(o_ref.dtype)

    @pl.when(j == pl.num_programs(1) - 1)
    def _():
        if zero_init:
            sfin_ref[0, 0] = s_ref[...]
        else:
            sfin_ref[0] = s_ref[...]


def _deltanet(q, k, v, gb, s0, states, layer, *, tt):
    bsz, t, _ = q.shape
    nt = t // tt
    nc = tt // CHUNK
    n = N_HEADS * CHUNK
    zero_init = s0 is None
    fwd = lambda c: pl.BlockSpec((1, tt, c), lambda b, j: (b, j, 0))
    bwd = lambda c: pl.BlockSpec((1, tt, c), lambda b, j: (b, nt - 1 - j, 0))
    st_block = (2, N_HEADS, HEAD_D, HEAD_D)
    in_specs = [fwd(D_A), fwd(D_A), fwd(D_A), fwd(LANES), bwd(D_A), bwd(D_A), bwd(D_A), bwd(LANES)]
    args = [q, k, v, gb, q, k, v, gb]
    if zero_init:
        in_specs.append(pl.BlockSpec(memory_space=pl.ANY))
        args.append(states)
        st_out_spec = pl.BlockSpec((1, 1) + st_block, lambda b, j: (b, layer, 0, 0, 0, 0))
        st_out_shape = jax.ShapeDtypeStruct(states.shape, F32)
        aliases = {len(args) - 1: 2}
    else:
        st_spec = pl.BlockSpec((1,) + st_block, lambda b, j: (b, 0, 0, 0, 0))
        in_specs.append(st_spec)
        args.append(s0)
        st_out_spec = st_spec
        st_out_shape = jax.ShapeDtypeStruct((bsz,) + st_block, F32)
        aliases = {}
    n_prob = 2 * nc
    return pl.pallas_call(
        functools.partial(_deltanet_kernel, nc=nc, zero_init=zero_init),
        grid=(bsz, nt),
        in_specs=in_specs,
        out_specs=[fwd(D_A), bwd(D_A), st_out_spec],
        out_shape=[
            jax.ShapeDtypeStruct((bsz, t, D_A), MM),
            jax.ShapeDtypeStruct((bsz, t, D_A), MM),
            st_out_shape,
        ],
        input_output_aliases=aliases,
        scratch_shapes=[
            pltpu.VMEM((n_prob, n, n), F32),
            pltpu.VMEM((n_prob, n, n), F32),
            pltpu.VMEM((n_prob, n, n), MM),
            pltpu.VMEM((n_prob, n, 2 * HEAD_D), MM),
            pltpu.VMEM((n_prob, n, HEAD_D), MM),
            pltpu.VMEM((n_prob, n, HEAD_D), MM),
            pltpu.VMEM((n_prob, 1, LANES), F32),
            pltpu.VMEM((n_prob, n, 2 * HEAD_D), F32),
            pltpu.VMEM((2, N_HEADS, HEAD_D, HEAD_D), F32),
        ],
        compiler_params=pltpu.CompilerParams(
            dimension_semantics=("parallel", "arbitrary"), vmem_limit_bytes=VMEM_LIMIT),
        name="deltanet",
    )(*args)


def _mixer_out_kernel(of_ref, ob_ref, z_ref, u_ref, x_ref, mod_ref, gon_ref, wo_ref, gpost_ref,
                      out_ref, cat_ref, mix_ref, *, tm):
    rb = ROWS
    gate = mod_ref[0, 2:3, :]
    for r in range(0, tm, rb):
        rows = slice(r, r + rb)
        o = of_ref[0, rows, :].astype(F32) + ob_ref[0, rows, :].astype(F32)
        z = z_ref[0, rows, :].astype(F32)
        for h in range(N_HEADS):
            hs = slice(h * HEAD_D, (h + 1) * HEAD_D)
            oh = o[:, hs]
            oh = oh * _rms_scale(oh) * gon_ref[0]
            cat_ref[rows, hs] = (oh * _silu(z[:, hs])).astype(MM)
        cat_ref[rows, D_A:] = u_ref[0, rows, :]
    mix_ref[...] = jnp.dot(cat_ref[...], wo_ref[0], preferred_element_type=F32)
    for r in range(0, tm, rb):
        rows = slice(r, r + rb)
        mix = mix_ref[rows, :]
        mix = mix * _rms_scale(mix) * gpost_ref[0]
        out_ref[0, rows, :] = x_ref[0, rows, :] + gate * mix


def _mixer_out(o_f, o_b, z, u, x, mod, mrow, layer, wts, *, tm):
    bsz, t, d = x.shape
    names = ("g_onorm", "w_out", "g_post_mix")
    seq = lambda c: pl.BlockSpec((1, tm, c), lambda b, i: (b, i, 0))
    return pl.pallas_call(
        functools.partial(_mixer_out_kernel, tm=tm),
        grid=(bsz, t // tm),
        in_specs=[
            seq(D_A), seq(D_A), seq(D_A), seq(D_B), seq(d),
            pl.BlockSpec((1, N_MOD, d), lambda b, i: (mrow(b), 0, 0)),
        ] + [_layer_spec(wts[n], layer) for n in names],
        out_specs=seq(d),
        out_shape=jax.ShapeDtypeStruct((bsz, t, d), F32),
        scratch_shapes=[pltpu.VMEM((tm, D_A + D_B), MM), pltpu.VMEM((tm, d), F32)],
        compiler_params=pltpu.CompilerParams(vmem_limit_bytes=VMEM_LIMIT),
        name="mixer_out",
    )(o_f, o_b, z, u, x, mod, *[wts[n] for n in names])


def _ffn_kernel(xp_ref, xc_ref, xn_ref, mod_ref, gpre_ref, wgu_ref, wd_ref, cw_ref,
                cb_ref, gpost_ref, out_ref, hext_ref, gts0_ref, gts1_ref, up0_ref, up1_ref,
                act0_ref, act1_ref, y_ref, *, tm, nt, halo, width, taps, n_chunks, unroll_all):
    i = pl.program_id(1)
    pad = SUBLANES
    shift = mod_ref[0, 3:4, :]
    scale1 = 1.0 + mod_ref[0, 4:5, :]
    gate = mod_ref[0, 5:6, :]
    gpre = gpre_ref[0]
    rb = ROWS
    n_ext = tm + 2 * halo

    def norm(x):
        return ((x * _rms_scale(x)) * gpre) * scale1 + shift

    if halo:
        hext_ref[0:halo, :] = jnp.where(i > 0, norm(xp_ref[0]), 0.0).astype(MM)
        hext_ref[halo + tm:n_ext, :] = jnp.where(i < nt - 1, norm(xn_ref[0]), 0.0).astype(MM)
    for r in range(0, tm, rb):
        hext_ref[halo + r:halo + r + rb, :] = norm(xc_ref[0, r:r + rb, :]).astype(MM)

    gts, ups, acts = (gts0_ref, gts1_ref), (up0_ref, up1_ref), (act0_ref, act1_ref)
    for g_ref in gts:
        g_ref[0:pad, :] = jnp.zeros((pad, FF_CHUNK), F32)
        g_ref[pad + n_ext:pad + n_ext + pad, :] = jnp.zeros((pad, FF_CHUNK), F32)
    y_ref[...] = jnp.zeros_like(y_ref)

    def project(c, slot):
        gts[slot][pad:pad + n_ext, :] = jnp.dot(hext_ref[...], wgu_ref[0, c],
                                                preferred_element_type=F32)
        ups[slot][...] = jnp.dot(hext_ref[halo:halo + tm, :], wgu_ref[0, n_chunks + c],
                                 preferred_element_type=F32)

    def activate(c, slot, rb=ACT_ROWS):
        for r in range(0, tm, rb):
            pos = (r + lax.broadcasted_iota(jnp.int32, (rb, 1), 0)) % width
            acc = None
            for dr, dc, widx in taps:
                s = pad + halo + r + dr * GRID_W + dc
                val = gts[slot][s:s + rb, :]
                if dc != 0:
                    ok = (pos >= 1) if dc < 0 else (pos <= width - 2)
                    val = jnp.where(ok, val, 0.0)
                term = val.reshape(rb // SUBLANES, SUBLANES, FF_CHUNK) * cw_ref[0, c, widx]
                acc = term if acc is None else acc + term
            gt = acc.reshape(rb, FF_CHUNK) + cb_ref[0, c]
            acts[slot][r:r + rb, :] = (_silu(gt) * ups[slot][r:r + rb, :]).astype(MM)

    def contract(c, slot):
        y_ref[...] += jnp.dot(acts[slot][...], wd_ref[0, c], preferred_element_type=F32)

    def steady(c, slot):
        activate(c, slot)
        project(c + 1, 1 - slot)
        contract(c, slot)

    project(0, 0)
    n_steady = n_chunks - 1
    if unroll_all:
        n_looped = 0
    else:
        per_iter = FFN_CHUNKS_PER_ITER
        n_looped = n_steady - n_steady % per_iter

        def loop_body(k, carry):
            for s in range(per_iter):
                steady(per_iter * k + s, s % 2)
            return carry

        lax.fori_loop(0, n_looped // per_iter, loop_body, 0)
    for c in range(n_looped, n_steady):
        steady(c, c % 2)
    last = n_chunks - 1
    activate(last, last % 2)
    contract(last, last % 2)

    for r in range(0, tm, rb):
        y = y_ref[r:r + rb, :]
        y = y * _rms_scale(y) * gpost_ref[0]
        out_ref[0, r:r + rb, :] = xc_ref[0, r:r + rb, :] + gate * y


def _ffn(x, mod, mrow, layer, wts, *, tm, grid_conv):
    bsz, t, d = x.shape
    nt = t // tm
    n_chunks = wts["w_down"].shape[1]
    if grid_conv:
        halo = GRID_W
        width = GRID_W
        taps = tuple((dr, dc, (dr + 1) * 3 + (dc + 1)) for dr in (-1, 0, 1) for dc in (-1, 0, 1))
        hb, nhb = tm // halo, t // halo
        xp_spec = pl.BlockSpec((1, halo, d), lambda b, i: (b, jnp.maximum(i * hb - 1, 0), 0))
        xn_spec = pl.BlockSpec((1, halo, d), lambda b, i: (b, jnp.minimum((i + 1) * hb, nhb - 1), 0))
    else:
        assert tm == t
        halo = 0
        width = t
        taps = tuple((0, dc, 3 + (dc + 1)) for dc in (-1, 0, 1))
        xp_spec = pl.BlockSpec((1, 8, d), lambda b, i: (b, 0, 0))
        xn_spec = pl.BlockSpec((1, 8, d), lambda b, i: (b, 0, 0))
    names = ("g_pre_ffn", "w_gu", "w_down", "conv_ffn", "b_conv_ffn", "g_post_ffn")
    return pl.pallas_call(
        functools.partial(_ffn_kernel, tm=tm, nt=nt, halo=halo, width=width, taps=taps,
                          n_chunks=n_chunks, unroll_all=tm <= FFN_UNROLL_MAX_ROWS),
        grid=(bsz, nt),
        in_specs=[
            xp_spec,
            pl.BlockSpec((1, tm, d), lambda b, i: (b, i, 0)),
            xn_spec,
            pl.BlockSpec((1, N_MOD, d), lambda b, i: (mrow(b), 0, 0)),
        ] + [_layer_spec(wts[n], layer) for n in names],
        out_specs=pl.BlockSpec((1, tm, d), lambda b, i: (b, i, 0)),
        out_shape=jax.ShapeDtypeStruct((bsz, t, d), F32),
        scratch_shapes=[
            pltpu.VMEM((tm + 2 * halo, d), MM),
            pltpu.VMEM((tm + 2 * halo + 2 * SUBLANES, FF_CHUNK), F32),
            pltpu.VMEM((tm + 2 * halo + 2 * SUBLANES, FF_CHUNK), F32),
            pltpu.VMEM((tm, FF_CHUNK), F32),
            pltpu.VMEM((tm, FF_CHUNK), F32),
            pltpu.VMEM((tm, FF_CHUNK), MM),
            pltpu.VMEM((tm, FF_CHUNK), MM),
            pltpu.VMEM((tm, d), F32),
        ],
        compiler_params=pltpu.CompilerParams(vmem_limit_bytes=VMEM_LIMIT),
        name="ffn",
    )(x, x, x, mod, *[wts[n] for n in names])


def _prepare_weights(w_in, conv_qkv, a_log, dt_bias, g_onorm, conv_cf, b_conv_cf, ln_cf_g,
                     ln_cf_b, w_out, w_up, conv_ffn, b_conv_ffn, w_down, g_pre_mix, g_post_mix,
                     g_pre_ffn, g_post_ffn):
    depth, d, _ = w_in.shape
    d_ff = w_down.shape[1]
    n_chunks = d_ff // FF_CHUNK
    n_ab = 4 * N_HEADS
    w_in = w_in.astype(MM)
    w_up = w_up.astype(MM)
    ab = w_in[:, :, 4 * D_A:4 * D_A + n_ab]
    vec = lambda a: a.reshape(depth, 1, -1)
    rows8 = lambda a: jnp.broadcast_to(a[..., None, :], a.shape[:-1] + (SUBLANES, a.shape[-1]))
    pad_lane = lambda a: jnp.pad(a.reshape(depth, 1, -1),
                                 ((0, 0), (0, 0), (0, LANES - 2 * N_HEADS)))
    return {
        "w_qkv": w_in[:, :, :3 * D_A],
        "w_z": w_in[:, :, 3 * D_A:4 * D_A],
        "w_ab": jnp.pad(ab, ((0, 0), (0, 0), (0, LANES - n_ab))),
        "w_glu": w_in[:, :, 4 * D_A + n_ab:],
        "conv_qkv": rows8(conv_qkv), "conv_cf": rows8(conv_cf), "b_conv_cf": vec(b_conv_cf),
        "ln_cf_g": vec(ln_cf_g), "ln_cf_b": vec(ln_cf_b),
        "a_log": pad_lane(a_log), "dt_bias": pad_lane(dt_bias),
        "g_onorm": vec(g_onorm), "w_out": w_out.astype(MM),
        "w_gu": jnp.transpose(w_up.reshape(depth, d, 2 * n_chunks, FF_CHUNK), (0, 2, 1, 3)),
        "w_down": w_down.astype(MM).reshape(depth, n_chunks, FF_CHUNK, d),
        "conv_ffn": rows8(jnp.transpose(
            conv_ffn.reshape(depth, 9, n_chunks, FF_CHUNK), (0, 2, 1, 3))),
        "b_conv_ffn": b_conv_ffn.reshape(depth, n_chunks, 1, FF_CHUNK),
        "g_pre_mix": vec(g_pre_mix), "g_post_mix": vec(g_post_mix),
        "g_pre_ffn": vec(g_pre_ffn), "g_post_ffn": vec(g_post_ffn),
    }


def _tiles(seq_len):
    tile = min(MAX_TILE_ROWS, seq_len)
    assert seq_len % tile == 0 and tile % (2 * CHUNK) == 0 and tile % GRID_W == 0
    return tile


def _trunk_layer(x, mod, mrow, s0, states, layer, wts, *, grid_conv):
    tile = _tiles(x.shape[1])
    q, k, v, z, gb, u = _mixer_in(x, mod, mrow, layer, wts, tm=tile)
    o_f, o_b, s_fin = _deltanet(q, k, v, gb, s0, states, layer, tt=tile)
    x = _mixer_out(o_f, o_b, z, u, x, mod, mrow, layer, wts, tm=tile)
    x = _ffn(x, mod, mrow, layer, wts, tm=tile, grid_conv=grid_conv)
    return x, s_fin


def kernel(x_prompt, x_sample, state_delta, c, c_ctx, w_mod, b_mod, g_pre_mix, g_post_mix,
           g_pre_ffn, g_post_ffn, w_in, conv_qkv, a_log, dt_bias, g_onorm, conv_cf, b_conv_cf,
           ln_cf_g, ln_cf_b, w_out, w_up, conv_ffn, b_conv_ffn, w_down):
    depth, d = g_pre_mix.shape
    dec_b = c.shape[0]
    assert dec_b < 8
    cc = jnp.concatenate([c, c_ctx[None, :], jnp.zeros((8 - dec_b - 1, d), c.dtype)], axis=0)
    mod = _modulation(cc, w_mod, b_mod).reshape(depth, 8, N_MOD, d)
    wts = _prepare_weights(w_in, conv_qkv, a_log, dt_bias, g_onorm, conv_cf, b_conv_cf, ln_cf_g,
                           ln_cf_b, w_out, w_up, conv_ffn, b_conv_ffn, w_down, g_pre_mix,
                           g_post_mix, g_pre_ffn, g_post_ffn)
    xp, xs = x_prompt, x_sample
    ctx_states = jnp.zeros((xp.shape[0], depth, 2, N_HEADS, HEAD_D, HEAD_D), F32)
    for layer in range(depth):
        xp, ctx_states = _trunk_layer(xp, mod[layer], lambda b: dec_b, None, ctx_states, layer,
                                      wts, grid_conv=False)
        xs, _ = _trunk_layer(xs, mod[layer], lambda b: b, state_delta[:, layer], None, layer,
                             wts, grid_conv=True)
    return xp, xs, ctx_states
```

```python
import functools

import jax
import jax.numpy as jnp
from jax import lax
from jax.experimental import pallas as pl
from jax.experimental.pallas import tpu as pltpu

F32 = jnp.float32
MM = jnp.bfloat16
EPS = 1e-6
GRID_W = 64
N_HEADS = 4
HEAD_D = 128
D_A = N_HEADS * HEAD_D
D_B = 512
SHORT_CONV = 5
CF_CONV = 31
CHUNK = 64
N_MOD = 6
HALO = 16
FF_CHUNK = 256
LANES = 128
SUBLANES = 8
SH_COLS = 512
MAX_TILE_ROWS = 512
ROWS = 64
FFN_UNROLL_MAX_ROWS = 256
FFN_CHUNKS_PER_ITER = 2
ACT_ROWS = 64
MOD_COLS = 1536
VMEM_LIMIT = 56 * 1024 * 1024


def _sigmoid(x):
    return 1.0 / (1.0 + jnp.exp(-x))


def _silu(x):
    return x * _sigmoid(x)


def _softplus(x):
    return jnp.maximum(x, 0.0) + jnp.log(1.0 + jnp.exp(-jnp.abs(x)))


def _dot(a, b):
    return jnp.dot(a.astype(MM), b.astype(MM), preferred_element_type=F32)


def _dot_nt(a, b):
    return lax.dot_general(a.astype(MM), b.astype(MM), (((1,), (1,)), ((), ())),
                           preferred_element_type=F32)


def _dot_tn(a, b):
    return lax.dot_general(a.astype(MM), b.astype(MM), (((0,), (0,)), ((), ())),
                           preferred_element_type=F32)


def _dot_exact(a, b):
    return jnp.dot(a, b, preferred_element_type=F32, precision=lax.Precision.HIGHEST)


def _rms_scale(x):
    return lax.rsqrt(jnp.mean(x * x, axis=-1, keepdims=True) + EPS)


def _mod_kernel(c_ref, w_ref, b_ref, o_ref):
    o_ref[0] = _dot_exact(_silu(c_ref[...]), w_ref[0]) + b_ref[0]


def _modulation(cc, w_mod, b_mod):
    depth, d, n = w_mod.shape
    tn = MOD_COLS
    return pl.pallas_call(
        _mod_kernel,
        grid=(depth, n // tn),
        in_specs=[
            pl.BlockSpec((8, d), lambda l, j: (0, 0)),
            pl.BlockSpec((1, d, tn), lambda l, j: (l, 0, j)),
            pl.BlockSpec((1, 1, tn), lambda l, j: (l, 0, j)),
        ],
        out_specs=pl.BlockSpec((1, 8, tn), lambda l, j: (l, 0, j)),
        out_shape=jax.ShapeDtypeStruct((depth, 8, n), F32),
        compiler_params=pltpu.CompilerParams(vmem_limit_bytes=VMEM_LIMIT),
        name="modulation",
    )(cc, w_mod, b_mod.reshape(depth, 1, n))


def _layer_spec(a, layer):
    return pl.BlockSpec((1,) + a.shape[1:], lambda b, i: (layer,) + (0,) * (a.ndim - 1))


def _mixer_in_kernel(xp_ref, xc_ref, xn_ref, mod_ref, gpre_ref, wqkv_ref, wglu_ref, wz_ref,
                     wab_ref, cq_ref, ccf_ref, bcf_ref, lng_ref, lnb_ref, alog_ref, dt_ref,
                     q_ref, k_ref, v_ref, z_ref, gb_ref, u_ref,
                     hext_ref, pext_ref, glu_ref, sh_ref, rot_ref, *, tm, nt):
    i = pl.program_id(1)
    shift = mod_ref[0, 0:1, :]
    scale1 = 1.0 + mod_ref[0, 1:2, :]
    gpre = gpre_ref[0]
    rb = ROWS
    n_ext = tm + 2 * HALO

    def norm(x):
        return ((x * _rms_scale(x)) * gpre) * scale1 + shift

    hext_ref[0:HALO, :] = jnp.where(i > 0, norm(xp_ref[0]), 0.0).astype(MM)
    hext_ref[HALO + tm:n_ext, :] = jnp.where(i < nt - 1, norm(xn_ref[0]), 0.0).astype(MM)
    for r in range(0, tm, rb):
        hext_ref[HALO + r:HALO + r + rb, :] = norm(xc_ref[0, r:r + rb, :]).astype(MM)

    hcur = hext_ref[HALO:HALO + tm, :]
    z_ref[0] = jnp.dot(hcur, wz_ref[0], preferred_element_type=F32).astype(z_ref.dtype)
    ab = jnp.dot(hcur, wab_ref[0], preferred_element_type=F32)
    g = -jnp.exp(alog_ref[0]) * _softplus(ab + dt_ref[0])
    col = lax.broadcasted_iota(jnp.int32, ab.shape, 1)
    gb_ref[0] = jnp.where(col < 2 * N_HEADS, g, _sigmoid(ab))

    pext_ref[:, 0:3 * D_A] = jnp.dot(hext_ref[...], wqkv_ref[0], preferred_element_type=F32)
    glu_ref[...] = jnp.dot(hext_ref[...], wglu_ref[0], preferred_element_type=F32)
    for r in range(0, n_ext, rb):
        rows = min(rb, n_ext - r)
        pext_ref[r:r + rows, 3 * D_A:] = (glu_ref[r:r + rows, 0:D_B]
                                          * _sigmoid(glu_ref[r:r + rows, D_B:]))

    def dwconv(c0, w_ref, wc0, width, r):
        acc = None
        for t in range(width):
            off = HALO + r + t - width // 2
            res = off % SUBLANES
            if res == 0:
                val = pext_ref[off:off + rb, c0:c0 + SH_COLS]
            else:
                val = sh_ref[res - 1, off - res:off - res + rb, :]
            term = val.reshape(rb // SUBLANES, SUBLANES, SH_COLS) * w_ref[0, t, :, wc0:wc0 + SH_COLS]
            acc = term if acc is None else acc + term
        return acc.reshape(rb, SH_COLS)

    def shifted_copies(c0, width):
        used = sorted({(HALO + t - width // 2) % SUBLANES for t in range(width)} - {0})
        groups = n_ext // SUBLANES
        x3 = pext_ref[:, c0:c0 + SH_COLS].reshape(groups, SUBLANES, SH_COLS)
        sub = lax.broadcasted_iota(jnp.int32, (groups - 1, SUBLANES, SH_COLS), 1)
        for s in used:
            rot_ref[...] = pltpu.roll(x3, SUBLANES - s, 1)
            out = jnp.where(sub < SUBLANES - s, rot_ref[0:groups - 1], rot_ref[1:groups])
            sh_ref[s - 1, :, :] = out.reshape(n_ext - SUBLANES, SH_COLS)

    for part, out_ref in enumerate((q_ref, k_ref, v_ref)):
        shifted_copies(part * D_A, SHORT_CONV)
        for r in range(0, tm, rb):
            acc = _silu(dwconv(part * D_A, cq_ref, part * D_A, SHORT_CONV, r))
            if part < 2:
                heads = []
                for h in range(N_HEADS):
                    a = acc[:, h * HEAD_D:(h + 1) * HEAD_D]
                    a = a * lax.rsqrt(jnp.sum(a * a, axis=-1, keepdims=True) + EPS)
                    heads.append(a * (HEAD_D ** -0.5) if part == 0 else a)
                acc = jnp.concatenate(heads, axis=-1)
            out_ref[0, r:r + rb, :] = acc

    shifted_copies(3 * D_A, CF_CONV)
    for r in range(0, tm, rb):
        acc = dwconv(3 * D_A, ccf_ref, 0, CF_CONV, r) + bcf_ref[0]
        mu = jnp.mean(acc, axis=-1, keepdims=True)
        cen = acc - mu
        var = jnp.mean(cen * cen, axis=-1, keepdims=True)
        y = cen * lax.rsqrt(var + EPS) * lng_ref[0] + lnb_ref[0]
        u_ref[0, r:r + rb, :] = _silu(y).astype(u_ref.dtype)


def _mixer_in(x, mod, mrow, layer, wts, *, tm):
    bsz, t, d = x.shape
    nt = t // tm
    hb = tm // HALO
    nhb = t // HALO
    names = ("g_pre_mix", "w_qkv", "w_glu", "w_z", "w_ab", "conv_qkv", "conv_cf", "b_conv_cf",
             "ln_cf_g", "ln_cf_b", "a_log", "dt_bias")
    seq = lambda c: pl.BlockSpec((1, tm, c), lambda b, i: (b, i, 0))
    return pl.pallas_call(
        functools.partial(_mixer_in_kernel, tm=tm, nt=nt),
        grid=(bsz, nt),
        in_specs=[
            pl.BlockSpec((1, HALO, d), lambda b, i: (b, jnp.maximum(i * hb - 1, 0), 0)),
            pl.BlockSpec((1, tm, d), lambda b, i: (b, i, 0)),
            pl.BlockSpec((1, HALO, d), lambda b, i: (b, jnp.minimum((i + 1) * hb, nhb - 1), 0)),
            pl.BlockSpec((1, N_MOD, d), lambda b, i: (mrow(b), 0, 0)),
        ] + [_layer_spec(wts[n], layer) for n in names],
        out_specs=[seq(D_A), seq(D_A), seq(D_A), seq(D_A), seq(LANES), seq(D_B)],
        out_shape=[
            jax.ShapeDtypeStruct((bsz, t, D_A), F32),
            jax.ShapeDtypeStruct((bsz, t, D_A), F32),
            jax.ShapeDtypeStruct((bsz, t, D_A), F32),
            jax.ShapeDtypeStruct((bsz, t, D_A), MM),
            jax.ShapeDtypeStruct((bsz, t, LANES), F32),
            jax.ShapeDtypeStruct((bsz, t, D_B), MM),
        ],
        scratch_shapes=[
            pltpu.VMEM((tm + 2 * HALO, d), MM),
            pltpu.VMEM((tm + 2 * HALO, 3 * D_A + D_B), F32),
            pltpu.VMEM((tm + 2 * HALO, 2 * D_B), F32),
            pltpu.VMEM((SUBLANES - 1, tm + 2 * HALO - SUBLANES, SH_COLS), F32),
            pltpu.VMEM(((tm + 2 * HALO) // SUBLANES, SUBLANES, SH_COLS), F32),
        ],
        compiler_params=pltpu.CompilerParams(vmem_limit_bytes=VMEM_LIMIT),
        name="mixer_in",
    )(x, x, x, mod, *[wts[n] for n in names])


def _dn_masks(d):
    n = N_HEADS * CHUNK
    ii = lax.broadcasted_iota(jnp.int32, (n, n), 0)
    jj = lax.broadcasted_iota(jnp.int32, (n, n), 1)
    lo, hi = (jj, ii) if d == 0 else (ii, jj)
    return ii, jj, lo, hi


def _dn_prep(q_ref, k_ref, v_ref, gb_ref, sq, r0, d, p, amat_ref, tinv_ref, qk_ref, rhs_ref,
             qg_ref, kd_ref, egt_ref):
    rows = slice(r0, r0 + CHUNK)
    log2_chunk = CHUNK.bit_length() - 1
    gbc = gb_ref[sq, rows, :]
    ci = lax.broadcasted_iota(jnp.int32, (CHUNK, CHUNK), 0)
    cj = lax.broadcasted_iota(jnp.int32, (CHUNK, CHUNK), 1)
    tri = (ci >= cj) if d == 0 else (ci <= cj)
    gc = _dot_exact(jnp.where(tri, 1.0, 0.0), gbc)
    last = CHUNK - 1 if d == 0 else 0
    gtot = gc[last:last + 1, :]
    egc = jnp.exp(gc)
    edk = jnp.exp(gtot - gc)
    egt_ref[p] = jnp.exp(gtot)
    gct = jnp.concatenate([gc, jnp.zeros_like(gc)], axis=0).T

    def stack_cols(a, c0):
        return jnp.concatenate([a[:, c0 + h:c0 + h + 1] for h in range(N_HEADS)], axis=0)

    c0 = d * N_HEADS
    gc_col = stack_cols(gc, c0)
    be_col = stack_cols(gbc, 2 * N_HEADS + c0)
    egc_col = stack_cols(egc, c0)
    edk_col = stack_cols(edk, c0)
    gc_row = jnp.concatenate([gct[c0 + h:c0 + h + 1, 0:CHUNK] for h in range(N_HEADS)], axis=1)

    def stack_heads(ref):
        return jnp.concatenate(
            [ref[sq, rows, h * HEAD_D:(h + 1) * HEAD_D] for h in range(N_HEADS)], axis=0)

    qs, ks, vs = stack_heads(q_ref), stack_heads(k_ref), stack_heads(v_ref)
    ii, jj, lo, hi = _dn_masks(d)
    same = (ii >> log2_chunk) == (jj >> log2_chunk)
    incl = jnp.logical_and(same, lo <= hi)
    decay = jnp.where(incl, jnp.exp(jnp.where(incl, gc_col - gc_row, 0.0)), 0.0)
    kb = ks * be_col
    amat = _dot_nt(kb, ks) * decay
    amat_ref[p] = amat
    pair = jnp.logical_and((lo >> 1) == (hi >> 1), lo < hi)
    tinv_ref[p] = jnp.where(ii == jj, 1.0, 0.0) - jnp.where(pair, amat, 0.0)
    qk_ref[p] = (_dot_nt(qs, ks) * decay).astype(MM)
    rhs_ref[p] = jnp.concatenate([vs * be_col, kb * egc_col], axis=1).astype(MM)
    qg_ref[p] = (qs * egc_col).astype(MM)
    kd_ref[p] = (ks * edk_col).astype(MM)


def _deltanet_kernel(*refs, nc, nseq, zero_init):
    if zero_init:
        qf, kf, vf, gf, qb, kb, vb, gb, _, of_ref, ob_ref, sfin_ref = refs[:12]
    else:
        qf, kf, vf, gf, qb, kb, vb, gb, s0_ref, of_ref, ob_ref, sfin_ref = refs[:12]
    amat_ref, tinv_ref, qk_ref, rhs_ref, qg_ref, kd_ref, egt_ref, uw_ref, s_ref = refs[-9:]
    j = pl.program_id(1)
    log2_chunk = CHUNK.bit_length() - 1
    n = N_HEADS * CHUNK

    @pl.when(j == 0)
    def _():
        if zero_init:
            s_ref[...] = jnp.zeros_like(s_ref)
        else:
            s_ref[...] = s0_ref[...]

    probs = []
    for c in range(nc):
        for sq in range(nseq):
            probs.append((0, c * CHUNK, (qf, kf, vf, gf), of_ref, sq))
            probs.append((1, (nc - 1 - c) * CHUNK, (qb, kb, vb, gb), ob_ref, sq))

    for p, (d, r0, (q_ref, k_ref, v_ref, g_ref), _, sq) in enumerate(probs):
        _dn_prep(q_ref, k_ref, v_ref, g_ref, sq, r0, d, p, amat_ref, tinv_ref, qk_ref, rhs_ref,
                 qg_ref, kd_ref, egt_ref)

    for lb in range(1, log2_chunk):
        e_masks = []
        for d in (0, 1):
            _, _, lo, hi = _dn_masks(d)
            e_masks.append(jnp.logical_and((lo >> (lb + 1)) == (hi >> (lb + 1)),
                                           (lo >> lb) < (hi >> lb)))
        b = 1 << lb
        for p, (d, _, _, _, _) in enumerate(probs):
            tb = tinv_ref[p].astype(MM)
            e = jnp.where(e_masks[d], amat_ref[p], 0.0)
            if b < SUBLANES:
                tinv_ref[p] = tinv_ref[p] - _dot(_dot(tb, e), tb)
            else:
                starts = [s + (b if d == 0 else 0) for s in range(0, n, 2 * b)]
                sel = jnp.concatenate([tinv_ref[p, s:s + b, :] for s in starts], axis=0)
                x = _dot(_dot(sel, e), tb)
                for idx, s in enumerate(starts):
                    tinv_ref[p, s:s + b, :] = sel[idx * b:(idx + 1) * b] - x[idx * b:(idx + 1) * b]

    for p in range(len(probs)):
        uw_ref[p] = _dot(tinv_ref[p], rhs_ref[p])

    for p, (d, r0, _, o_ref, sq) in enumerate(probs):
        c0 = d * N_HEADS
        vnew, o1 = [], []
        for h in range(N_HEADS):
            hs = slice(h * CHUNK, (h + 1) * CHUNK)
            wq = jnp.concatenate([uw_ref[p, hs, HEAD_D:].astype(MM), qg_ref[p, hs, :]], axis=0)
            r = _dot(wq, s_ref[sq, d, h])
            vnew.append(uw_ref[p, hs, 0:HEAD_D] - r[:CHUNK])
            o1.append(r[CHUNK:])
        vnew = jnp.concatenate(vnew, axis=0).astype(MM)
        o = jnp.concatenate(o1, axis=0) + _dot(qk_ref[p], vnew)
        for h in range(N_HEADS):
            hs = slice(h * CHUNK, (h + 1) * CHUNK)
            s_ref[sq, d, h] = (s_ref[sq, d, h] * egt_ref[p, :, c0 + h:c0 + h + 1]
                               + _dot_tn(kd_ref[p, hs, :], vnew[hs]))
            o_ref[sq, r0:r0 + CHUNK, h * HEAD_D:(h + 1) * HEAD_D] = o[hs].astype(o_ref.dtype)

    @pl.when(j == pl.num_programs(1) - 1)
    def _():
        if zero_init:
            sfin_ref[:, 0] = s_ref[...]
        else:
            sfin_ref[...] = s_ref[...]


def _deltanet(q, k, v, gb, s0, states, layer, *, tt):
    bsz, t, _ = q.shape
    nt = t // tt
    nc = tt // CHUNK
    n = N_HEADS * CHUNK
    zero_init = s0 is None
    nseq = max(1, min(bsz, MAX_TILE_ROWS // tt))
    assert bsz % nseq == 0
    fwd = lambda c: pl.BlockSpec((nseq, tt, c), lambda b, j: (b, j, 0))
    bwd = lambda c: pl.BlockSpec((nseq, tt, c), lambda b, j: (b, nt - 1 - j, 0))
    st_block = (2, N_HEADS, HEAD_D, HEAD_D)
    in_specs = [fwd(D_A), fwd(D_A), fwd(D_A), fwd(LANES), bwd(D_A), bwd(D_A), bwd(D_A), bwd(LANES)]
    args = [q, k, v, gb, q, k, v, gb]
    if zero_init:
        in_specs.append(pl.BlockSpec(memory_space=pl.ANY))
        args.append(states)
        st_out_spec = pl.BlockSpec((nseq, 1) + st_block, lambda b, j: (b, layer, 0, 0, 0, 0))
        st_out_shape = jax.ShapeDtypeStruct(states.shape, F32)
        aliases = {len(args) - 1: 2}
    else:
        st_spec = pl.BlockSpec((nseq,) + st_block, lambda b, j: (b, 0, 0, 0, 0))
        in_specs.append(st_spec)
        args.append(s0)
        st_out_spec = st_spec
        st_out_shape = jax.ShapeDtypeStruct((bsz,) + st_block, F32)
        aliases = {}
    n_prob = 2 * nc * nseq
    return pl.pallas_call(
        functools.partial(_deltanet_kernel, nc=nc, nseq=nseq, zero_init=zero_init),
        grid=(bsz // nseq, nt),
        in_specs=in_specs,
        out_specs=[fwd(D_A), bwd(D_A), st_out_spec],
        out_shape=[
            jax.ShapeDtypeStruct((bsz, t, D_A), MM),
            jax.ShapeDtypeStruct((bsz, t, D_A), MM),
            st_out_shape,
        ],
        input_output_aliases=aliases,
        scratch_shapes=[
            pltpu.VMEM((n_prob, n, n), F32),
            pltpu.VMEM((n_prob, n, n), F32),
            pltpu.VMEM((n_prob, n, n), MM),
            pltpu.VMEM((n_prob, n, 2 * HEAD_D), MM),
            pltpu.VMEM((n_prob, n, HEAD_D), MM),
            pltpu.VMEM((n_prob, n, HEAD_D), MM),
            pltpu.VMEM((n_prob, 1, LANES), F32),
            pltpu.VMEM((n_prob, n, 2 * HEAD_D), F32),
            pltpu.VMEM((nseq,) + st_block, F32),
        ],
        compiler_params=pltpu.CompilerParams(
            dimension_semantics=("parallel", "arbitrary"), vmem_limit_bytes=VMEM_LIMIT),
        name="deltanet",
    )(*args)


def _mixer_out_kernel(of_ref, ob_ref, z_ref, u_ref, x_ref, mod_ref, gon_ref, wo_ref, gpost_ref,
                      out_ref, cat_ref, mix_ref, *, tm):
    rb = ROWS
    gate = mod_ref[0, 2:3, :]
    for r in range(0, tm, rb):
        rows = slice(r, r + rb)
        o = of_ref[0, rows, :].astype(F32) + ob_ref[0, rows, :].astype(F32)
        z = z_ref[0, rows, :].astype(F32)
        for h in range(N_HEADS):
            hs = slice(h * HEAD_D, (h + 1) * HEAD_D)
            oh = o[:, hs]
            oh = oh * _rms_scale(oh) * gon_ref[0]
            cat_ref[rows, hs] = (oh * _silu(z[:, hs])).astype(MM)
        cat_ref[rows, D_A:] = u_ref[0, rows, :]
    mix_ref[...] = jnp.dot(cat_ref[...], wo_ref[0], preferred_element_type=F32)
    for r in range(0, tm, rb):
        rows = slice(r, r + rb)
        mix = mix_ref[rows, :]
        mix = mix * _rms_scale(mix) * gpost_ref[0]
        out_ref[0, rows, :] = x_ref[0, rows, :] + gate * mix


def _mixer_out(o_f, o_b, z, u, x, mod, mrow, layer, wts, *, tm):
    bsz, t, d = x.shape
    names = ("g_onorm", "w_out", "g_post_mix")
    seq = lambda c: pl.BlockSpec((1, tm, c), lambda b, i: (b, i, 0))
    return pl.pallas_call(
        functools.partial(_mixer_out_kernel, tm=tm),
        grid=(bsz, t // tm),
        in_specs=[
            seq(D_A), seq(D_A), seq(D_A), seq(D_B), seq(d),
            pl.BlockSpec((1, N_MOD, d), lambda b, i: (mrow(b), 0, 0)),
        ] + [_layer_spec(wts[n], layer) for n in names],
        out_specs=seq(d),
        out_shape=jax.ShapeDtypeStruct((bsz, t, d), F32),
        scratch_shapes=[pltpu.VMEM((tm, D_A + D_B), MM), pltpu.VMEM((tm, d), F32)],
        compiler_params=pltpu.CompilerParams(vmem_limit_bytes=VMEM_LIMIT),
        name="mixer_out",
    )(o_f, o_b, z, u, x, mod, *[wts[n] for n in names])


def _ffn_kernel(xp_ref, xc_ref, xn_ref, mod_ref, gpre_ref, wgu_ref, wd_ref, cw_ref,
                cb_ref, gpost_ref, out_ref, hext_ref, gts0_ref, gts1_ref, up0_ref, up1_ref,
                act0_ref, act1_ref, y_ref, *, tm, nt, halo, width, taps, n_chunks, unroll_all):
    i = pl.program_id(1)
    pad = SUBLANES
    shift = mod_ref[0, 3:4, :]
    scale1 = 1.0 + mod_ref[0, 4:5, :]
    gate = mod_ref[0, 5:6, :]
    gpre = gpre_ref[0]
    rb = ROWS
    n_ext = tm + 2 * halo

    def norm(x):
        return ((x * _rms_scale(x)) * gpre) * scale1 + shift

    if halo:
        hext_ref[0:halo, :] = jnp.where(i > 0, norm(xp_ref[0]), 0.0).astype(MM)
        hext_ref[halo + tm:n_ext, :] = jnp.where(i < nt - 1, norm(xn_ref[0]), 0.0).astype(MM)
    for r in range(0, tm, rb):
        hext_ref[halo + r:halo + r + rb, :] = norm(xc_ref[0, r:r + rb, :]).astype(MM)

    gts, ups, acts = (gts0_ref, gts1_ref), (up0_ref, up1_ref), (act0_ref, act1_ref)
    for g_ref in gts:
        g_ref[0:pad, :] = jnp.zeros((pad, FF_CHUNK), F32)
        g_ref[pad + n_ext:pad + n_ext + pad, :] = jnp.zeros((pad, FF_CHUNK), F32)
    y_ref[...] = jnp.zeros_like(y_ref)

    def project(c, slot):
        gts[slot][pad:pad + n_ext, :] = jnp.dot(hext_ref[...], wgu_ref[0, c],
                                                preferred_element_type=F32)
        ups[slot][...] = jnp.dot(hext_ref[halo:halo + tm, :], wgu_ref[0, n_chunks + c],
                                 preferred_element_type=F32)

    def activate(c, slot, rb=ACT_ROWS):
        for r in range(0, tm, rb):
            pos = (r + lax.broadcasted_iota(jnp.int32, (rb, 1), 0)) % width
            acc = None
            for dr, dc, widx in taps:
                s = pad + halo + r + dr * GRID_W + dc
                val = gts[slot][s:s + rb, :]
                if dc != 0:
                    ok = (pos >= 1) if dc < 0 else (pos <= width - 2)
                    val = jnp.where(ok, val, 0.0)
                term = val.reshape(rb // SUBLANES, SUBLANES, FF_CHUNK) * cw_ref[0, c, widx]
                acc = term if acc is None else acc + term
            gt = acc.reshape(rb, FF_CHUNK) + cb_ref[0, c]
            acts[slot][r:r + rb, :] = (_silu(gt) * ups[slot][r:r + rb, :]).astype(MM)

    def contract(c, slot):
        y_ref[...] += jnp.dot(acts[slot][...], wd_ref[0, c], preferred_element_type=F32)

    def steady(c, slot):
        activate(c, slot)
        project(c + 1, 1 - slot)
        contract(c, slot)

    project(0, 0)
    n_steady = n_chunks - 1
    if unroll_all:
        n_looped = 0
    else:
        per_iter = FFN_CHUNKS_PER_ITER
        n_looped = n_steady - n_steady % per_iter

        def loop_body(k, carry):
            for s in range(per_iter):
                steady(per_iter * k + s, s % 2)
            return carry

        lax.fori_loop(0, n_looped // per_iter, loop_body, 0)
    for c in range(n_looped, n_steady):
        steady(c, c % 2)
    last = n_chunks - 1
    activate(last, last % 2)
    contract(last, last % 2)

    for r in range(0, tm, rb):
        y = y_ref[r:r + rb, :]
        y = y * _rms_scale(y) * gpost_ref[0]
        out_ref[0, r:r + rb, :] = xc_ref[0, r:r + rb, :] + gate * y


def _ffn(x, mod, mrow, layer, wts, *, tm, grid_conv):
    bsz, t, d = x.shape
    nt = t // tm
    n_chunks = wts["w_down"].shape[1]
    if grid_conv:
        halo = GRID_W
        width = GRID_W
        taps = tuple((dr, dc, (dr + 1) * 3 + (dc + 1)) for dr in (-1, 0, 1) for dc in (-1, 0, 1))
        hb, nhb = tm // halo, t // halo
        xp_spec = pl.BlockSpec((1, halo, d), lambda b, i: (b, jnp.maximum(i * hb - 1, 0), 0))
        xn_spec = pl.BlockSpec((1, halo, d), lambda b, i: (b, jnp.minimum((i + 1) * hb, nhb - 1), 0))
    else:
        assert tm == t
        halo = 0
        width = t
        taps = tuple((0, dc, 3 + (dc + 1)) for dc in (-1, 0, 1))
        xp_spec = pl.BlockSpec((1, 8, d), lambda b, i: (b, 0, 0))
        xn_spec = pl.BlockSpec((1, 8, d), lambda b, i: (b, 0, 0))
    names = ("g_pre_ffn", "w_gu", "w_down", "conv_ffn", "b_conv_ffn", "g_post_ffn")
    return pl.pallas_call(
        functools.partial(_ffn_kernel, tm=tm, nt=nt, halo=halo, width=width, taps=taps,
                          n_chunks=n_chunks, unroll_all=tm <= FFN_UNROLL_MAX_ROWS),
        grid=(bsz, nt),
        in_specs=[
            xp_spec,
            pl.BlockSpec((1, tm, d), lambda b, i: (b, i, 0)),
            xn_spec,
            pl.BlockSpec((1, N_MOD, d), lambda b, i: (mrow(b), 0, 0)),
        ] + [_layer_spec(wts[n], layer) for n in names],
        out_specs=pl.BlockSpec((1, tm, d), lambda b, i: (b, i, 0)),
        out_shape=jax.ShapeDtypeStruct((bsz, t, d), F32),
        scratch_shapes=[
            pltpu.VMEM((tm + 2 * halo, d), MM),
            pltpu.VMEM((tm + 2 * halo + 2 * SUBLANES, FF_CHUNK), F32),
            pltpu.VMEM((tm + 2 * halo + 2 * SUBLANES, FF_CHUNK), F32),
            pltpu.VMEM((tm, FF_CHUNK), F32),
            pltpu.VMEM((tm, FF_CHUNK), F32),
            pltpu.VMEM((tm, FF_CHUNK), MM),
            pltpu.VMEM((tm, FF_CHUNK), MM),
            pltpu.VMEM((tm, d), F32),
        ],
        compiler_params=pltpu.CompilerParams(vmem_limit_bytes=VMEM_LIMIT),
        name="ffn",
    )(x, x, x, mod, *[wts[n] for n in names])


def _prepare_weights(w_in, conv_qkv, a_log, dt_bias, g_onorm, conv_cf, b_conv_cf, ln_cf_g,
                     ln_cf_b, w_out, w_up, conv_ffn, b_conv_ffn, w_down, g_pre_mix, g_post_mix,
                     g_pre_ffn, g_post_ffn):
    depth, d, _ = w_in.shape
    d_ff = w_down.shape[1]
    n_chunks = d_ff // FF_CHUNK
    n_ab = 4 * N_HEADS
    w_in = w_in.astype(MM)
    w_up = w_up.astype(MM)
    ab = w_in[:, :, 4 * D_A:4 * D_A + n_ab]
    vec = lambda a: a.reshape(depth, 1, -1)
    rows8 = lambda a: jnp.broadcast_to(a[..., None, :], a.shape[:-1] + (SUBLANES, a.shape[-1]))
    pad_lane = lambda a: jnp.pad(a.reshape(depth, 1, -1),
                                 ((0, 0), (0, 0), (0, LANES - 2 * N_HEADS)))
    return {
        "w_qkv": w_in[:, :, :3 * D_A],
        "w_z": w_in[:, :, 3 * D_A:4 * D_A],
        "w_ab": jnp.pad(ab, ((0, 0), (0, 0), (0, LANES - n_ab))),
        "w_glu": w_in[:, :, 4 * D_A + n_ab:],
        "conv_qkv": rows8(conv_qkv), "conv_cf": rows8(conv_cf), "b_conv_cf": vec(b_conv_cf),
        "ln_cf_g": vec(ln_cf_g), "ln_cf_b": vec(ln_cf_b),
        "a_log": pad_lane(a_log), "dt_bias": pad_lane(dt_bias),
        "g_onorm": vec(g_onorm), "w_out": w_out.astype(MM),
        "w_gu": jnp.transpose(w_up.reshape(depth, d, 2 * n_chunks, FF_CHUNK), (0, 2, 1, 3)),
        "w_down": w_down.astype(MM).reshape(depth, n_chunks, FF_CHUNK, d),
        "conv_ffn": rows8(jnp.transpose(
            conv_ffn.reshape(depth, 9, n_chunks, FF_CHUNK), (0, 2, 1, 3))),
        "b_conv_ffn": b_conv_ffn.reshape(depth, n_chunks, 1, FF_CHUNK),
        "g_pre_mix": vec(g_pre_mix), "g_post_mix": vec(g_post_mix),
        "g_pre_ffn": vec(g_pre_ffn), "g_post_ffn": vec(g_post_ffn),
    }


def _tiles(seq_len):
    tile = min(MAX_TILE_ROWS, seq_len)
    assert seq_len % tile == 0 and tile % (2 * CHUNK) == 0 and tile % GRID_W == 0
    return tile


def _trunk_layer(x, mod, mrow, s0, states, layer, wts, *, grid_conv):
    tile = _tiles(x.shape[1])
    q, k, v, z, gb, u = _mixer_in(x, mod, mrow, layer, wts, tm=tile)
    o_f, o_b, s_fin = _deltanet(q, k, v, gb, s0, states, layer, tt=tile)
    x = _mixer_out(o_f, o_b, z, u, x, mod, mrow, layer, wts, tm=tile)
    x = _ffn(x, mod, mrow, layer, wts, tm=tile, grid_conv=grid_conv)
    return x, s_fin


def kernel(x_prompt, x_sample, state_delta, c, c_ctx, w_mod, b_mod, g_pre_mix, g_post_mix,
           g_pre_ffn, g_post_ffn, w_in, conv_qkv, a_log, dt_bias, g_onorm, conv_cf, b_conv_cf,
           ln_cf_g, ln_cf_b, w_out, w_up, conv_ffn, b_conv_ffn, w_down):
    depth, d = g_pre_mix.shape
    dec_b = c.shape[0]
    assert dec_b < 8
    cc = jnp.concatenate([c, c_ctx[None, :], jnp.zeros((8 - dec_b - 1, d), c.dtype)], axis=0)
    mod = _modulation(cc, w_mod, b_mod).reshape(depth, 8, N_MOD, d)
    wts = _prepare_weights(w_in, conv_qkv, a_log, dt_bias, g_onorm, conv_cf, b_conv_cf, ln_cf_g,
                           ln_cf_b, w_out, w_up, conv_ffn, b_conv_ffn, w_down, g_pre_mix,
                           g_post_mix, g_pre_ffn, g_post_ffn)
    xp, xs = x_prompt, x_sample
    ctx_states = jnp.zeros((xp.shape[0], depth, 2, N_HEADS, HEAD_D, HEAD_D), F32)
    for layer in range(depth):
        xp, ctx_states = _trunk_layer(xp, mod[layer], lambda b: dec_b, None, ctx_states, layer,
                                      wts, grid_conv=False)
        xs, _ = _trunk_layer(xs, mod[layer], lambda b: b, state_delta[:, layer], None, layer,
                             wts, grid_conv=True)
    return xp, xs, ctx_states
```
